```python
import math
import jax, jax.numpy as jnp
from jax import lax
import numpy as np

D_MODEL = 1024
BATCH = 8
SEQ = 4096
DEPTH = 1

CHUNK = 64
Q_BLOCK = 128
CONV_WIDTH = D_MODEL // 2
CONV_K = 3
DA_HEADS = 4
DA_HEAD_DIM = 64
DA_V_DIM = 2 * DA_HEAD_DIM
DA_QK_WIDTH = 2 * DA_HEADS * DA_HEAD_DIM
DA_V_WIDTH = DA_HEADS * DA_V_DIM
D_FF = 2816
ROPE_THETA = 10000.0
NORM_EPS = 1e-6
LAMBDA_STD = 0.1
IN_SPLITS = [CONV_WIDTH, CONV_WIDTH, CONV_WIDTH, DA_QK_WIDTH, DA_QK_WIDTH, DA_V_WIDTH, D_MODEL, D_MODEL]
IN_WIDTH = sum(IN_SPLITS)

kernel_name = "hybrid_conv_diffattn_macaron_block"


def rms_norm(x, g):
    xf = x.astype(jnp.float32)
    y = xf * lax.rsqrt(jnp.mean(xf * xf, axis=-1, keepdims=True) + NORM_EPS)
    return (y * g.astype(jnp.float32)).astype(x.dtype)


def swiglu(x, w_gate, w_up, w_down):
    return (jax.nn.silu(x @ w_gate) * (x @ w_up)) @ w_down


def rope_tables(seq, dim):
    inv_freq = 1.0 / (ROPE_THETA ** (jnp.arange(0, dim, 2, dtype=jnp.float32) / dim))
    ang = jnp.arange(seq, dtype=jnp.float32)[:, None] * inv_freq[None, :]
    return jnp.cos(ang)[:, None, :], jnp.sin(ang)[:, None, :]


def apply_rope(x, cos, sin):
    xf = x.astype(jnp.float32)
    x1, x2 = jnp.split(xf, 2, axis=-1)
    out = jnp.concatenate([x1 * cos - x2 * sin, x2 * cos + x1 * sin], axis=-1)
    return out.astype(x.dtype)


def short_gated_conv(xc, b_gate, c_gate, conv_w):
    z = c_gate * xc
    rhs = conv_w[:, None, :].astype(z.dtype)
    zc = lax.conv_general_dilated(z, rhs, window_strides=(1,), padding=[(CONV_K - 1, 0)],
                                  dimension_numbers=('NWC', 'WIO', 'NWC'),
                                  feature_group_count=z.shape[-1])
    return b_gate * zc


def diff_attention(q, k, v, lam):
    bsz, seq = q.shape[0], q.shape[1]
    nb = seq // Q_BLOCK
    scale = 1.0 / math.sqrt(DA_HEAD_DIM)
    k_chunk = jnp.arange(seq) // CHUNK
    qb = q.reshape(bsz, nb, Q_BLOCK, 2 * DA_HEADS, DA_HEAD_DIM).transpose(1, 0, 2, 3, 4)

    def block(args):
        q_blk, blk_idx = args
        s = jnp.einsum('bqhd,bkhd->bhqk', q_blk, k).astype(jnp.float32) * scale
        q_chunk = (blk_idx * Q_BLOCK + jnp.arange(Q_BLOCK)) // CHUNK
        mask = k_chunk[None, :] <= q_chunk[:, None]
        s = jnp.where(mask[None, None], s, -jnp.inf)
        p = jax.nn.softmax(s, axis=-1).reshape(bsz, DA_HEADS, 2, Q_BLOCK, seq)
        a = p[:, :, 0] - lam * p[:, :, 1]
        return jnp.einsum('bhqk,bkhe->bqhe', a.astype(v.dtype), v)

    o = lax.map(block, (qb, jnp.arange(nb)))
    return o.transpose(1, 0, 2, 3, 4).reshape(bsz, seq, DA_HEADS, DA_V_DIM)


def setup_inputs(seed: int = 0) -> dict:
    key = jax.random.key(seed)
    ks = jax.random.split(key, 24)
    f32 = jnp.float32

    def w(k, shape, fan_in):
        return jax.random.normal(k, shape, f32) * (fan_in ** -0.5)

    def gain(k, shape):
        return 1.0 + 0.02 * jax.random.normal(k, shape, f32)

    L = DEPTH
    return {
        "x": jax.random.normal(ks[0], (BATCH, SEQ, D_MODEL), f32),
        "norm_ffn1": gain(ks[1], (L, D_MODEL)),
        "ffn1_gate": w(ks[2], (L, D_MODEL, D_FF), D_MODEL),
        "ffn1_up": w(ks[3], (L, D_MODEL, D_FF), D_MODEL),
        "ffn1_down": w(ks[4], (L, D_FF, D_MODEL), D_FF),
        "norm_mix": gain(ks[5], (L, D_MODEL)),
        "w_in": w(ks[6], (L, D_MODEL, IN_WIDTH), D_MODEL),
        "b_gate": 0.01 * jax.random.normal(ks[7], (L, 2 * D_MODEL), f32),
        "conv_w": w(ks[8], (L, CONV_K, CONV_WIDTH), CONV_K),
        "w_conv_out": w(ks[9], (L, CONV_WIDTH, D_MODEL), CONV_WIDTH),
        "lambda_q1": LAMBDA_STD * jax.random.normal(ks[10], (L, DA_HEAD_DIM), f32),
        "lambda_k1": LAMBDA_STD * jax.random.normal(ks[11], (L, DA_HEAD_DIM), f32),
        "lambda_q2": LAMBDA_STD * jax.random.normal(ks[12], (L, DA_HEAD_DIM), f32),
        "lambda_k2": LAMBDA_STD * jax.random.normal(ks[13], (L, DA_HEAD_DIM), f32),
        "subln_g": gain(ks[14], (L, DA_V_DIM)),
        "w_attn_out": w(ks[15], (L, DA_V_WIDTH, D_MODEL), DA_V_WIDTH),
        "w_o": w(ks[16], (L, D_MODEL, D_MODEL), D_MODEL),
        "norm_ffn2": gain(ks[17], (L, D_MODEL)),
        "ffn2_gate": w(ks[18], (L, D_MODEL, D_FF), D_MODEL),
        "ffn2_up": w(ks[19], (L, D_MODEL, D_FF), D_MODEL),
        "ffn2_down": w(ks[20], (L, D_FF, D_MODEL), D_FF),
        "norm_final": gain(ks[21], (D_MODEL,)),
    }


def reference(x, norm_ffn1, ffn1_gate, ffn1_up, ffn1_down, norm_mix, w_in, b_gate, conv_w,
              w_conv_out, lambda_q1, lambda_k1, lambda_q2, lambda_k2, subln_g, w_attn_out, w_o,
              norm_ffn2, ffn2_gate, ffn2_up, ffn2_down, norm_final):
    bsz, seq, _ = x.shape
    cos, sin = rope_tables(seq, DA_HEAD_DIM)
    split_pts = list(np.cumsum(IN_SPLITS)[:-1])
    h = x
    for l in range(DEPTH):
        h = h + 0.5 * swiglu(rms_norm(h, norm_ffn1[l]), ffn1_gate[l], ffn1_up[l], ffn1_down[l])

        u = rms_norm(h, norm_mix[l])
        proj = u @ w_in[l]
        xc, bg, cg, q, k, v, g_conv, g_attn = jnp.split(proj, split_pts, axis=-1)

        y_conv = short_gated_conv(xc, bg, cg, conv_w[l]) @ w_conv_out[l]

        q = apply_rope(q.reshape(bsz, seq, 2 * DA_HEADS, DA_HEAD_DIM), cos, sin)
        k = apply_rope(k.reshape(bsz, seq, 2 * DA_HEADS, DA_HEAD_DIM), cos, sin)
        v = v.reshape(bsz, seq, DA_HEADS, DA_V_DIM)
        lambda_init = 0.8 - 0.6 * math.exp(-0.3 * l)
        lam = (jnp.exp(jnp.sum(lambda_q1[l].astype(jnp.float32) * lambda_k1[l].astype(jnp.float32)))
               - jnp.exp(jnp.sum(lambda_q2[l].astype(jnp.float32) * lambda_k2[l].astype(jnp.float32)))
               + lambda_init)
        o = diff_attention(q, k, v, lam)
        o = rms_norm(o, subln_g[l]) * (1.0 - lambda_init)
        y_attn = o.reshape(bsz, seq, DA_V_WIDTH) @ w_attn_out[l]

        g_conv = jax.nn.sigmoid(g_conv + b_gate[l, :D_MODEL])
        g_attn = jax.nn.sigmoid(g_attn + b_gate[l, D_MODEL:])
        h = h + (g_conv * y_conv + g_attn * y_attn) @ w_o[l]

        h = h + 0.5 * swiglu(rms_norm(h, norm_ffn2[l]), ffn2_gate[l], ffn2_up[l], ffn2_down[l])
    return rms_norm(h, norm_final)
```

```python
import functools
import math

import jax
import jax.numpy as jnp
from jax import lax
from jax.experimental import pallas as pl
from jax.experimental.pallas import tpu as pltpu

D_MODEL = 1024
CHUNK = 64
CONV_WIDTH = 512
CONV_K = 3
DA_HEADS = 4
DA_HEAD_DIM = 64
DA_V_DIM = 128
QK_WIDTH = 512
V_WIDTH = 512
D_FF = 2816
ROPE_THETA = 10000.0
NORM_EPS = 1e-6
LAMBDA_INIT = 0.8 - 0.6 * math.exp(-0.3 * 0)

OFF_XC, OFF_BG, OFF_CG, OFF_Q, OFF_K, OFF_V, OFF_GC, OFF_GA = (
    0, 512, 1024, 1536, 2048, 2560, 3072, 4096)
IN_WIDTH = 5120

V7X_VMEM_LIMIT_BYTES = 56 * 1024 * 1024

TM_FFN = 512
TM_PROJ = 512
TM_MERGE = 512
TQ = 512
TK = 512
CONV_HALO = 8

BF16 = jnp.bfloat16
F32 = jnp.float32


def _const_spec(shape):
    return pl.BlockSpec(shape, lambda *_: (0,) * len(shape), pipeline_mode=pl.Buffered(1))


def _rms(x, g):
    return x * lax.rsqrt(jnp.mean(x * x, axis=-1, keepdims=True) + NORM_EPS) * g


def _sigmoid(x):
    return 1.0 / (1.0 + jnp.exp(-x))


def _mm(a, b):
    return jnp.dot(a, b, preferred_element_type=F32)


def _ffn_kernel(x_ref, g_ref, wg_ref, wu_ref, wd_ref, gf_ref, o_ref, *, final_norm):
    x = x_ref[...]
    xn = _rms(x, g_ref[...]).astype(BF16)
    gate = _mm(xn, wg_ref[...])
    up = _mm(xn, wu_ref[...])
    a = (gate * _sigmoid(gate) * up).astype(BF16)
    y = x + 0.5 * _mm(a, wd_ref[...])
    if final_norm:
        y = _rms(y, gf_ref[...])
    o_ref[...] = y


def _ffn(x2d, g, wg, wu, wd, gf, *, final_norm):
    t = x2d.shape[0]
    tok = lambda i: (i, 0)
    return pl.pallas_call(
        functools.partial(_ffn_kernel, final_norm=final_norm),
        grid=(t // TM_FFN,),
        in_specs=[
            pl.BlockSpec((TM_FFN, D_MODEL), tok),
            _const_spec((1, D_MODEL)),
            _const_spec((D_MODEL, D_FF)),
            _const_spec((D_MODEL, D_FF)),
            _const_spec((D_FF, D_MODEL)),
            _const_spec((1, D_MODEL)),
        ],
        out_specs=pl.BlockSpec((TM_FFN, D_MODEL), tok),
        out_shape=jax.ShapeDtypeStruct((t, D_MODEL), F32),
        name="ffn_final" if final_norm else "ffn",
        compiler_params=pltpu.CompilerParams(
            dimension_semantics=("arbitrary",), vmem_limit_bytes=V7X_VMEM_LIMIT_BYTES),
    )(x2d, g, wg, wu, wd, gf)


def _rope(x, cos, sin_signed, lo_half):
    outs = []
    for c in range(x.shape[1] // 128):
        xc = x[:, c * 128:(c + 1) * 128]
        rot = jnp.where(lo_half, pltpu.roll(xc, 96, 1), pltpu.roll(xc, 32, 1))
        outs.append(xc * cos + rot * sin_signed)
    return jnp.concatenate(outs, axis=1)


def _in_proj_kernel(h_ref, g_ref, w_ref, bgate_ref, convw_ref, wco_ref, cosq_ref, sinq_ref,
                    cosk_ref, sink_ref, gy_ref, ga_ref, q_ref, k_ref, v_ref, zbuf_ref):
    si = pl.program_id(1)
    tm = h_ref.shape[1]
    u = _rms(h_ref[0], g_ref[...]).astype(BF16)

    def proj(off, width):
        return _mm(u, w_ref[:, off:off + width])

    z = proj(OFF_CG, CONV_WIDTH) * proj(OFF_XC, CONV_WIDTH)

    @pl.when(si == 0)
    def _():
        zbuf_ref[0:CONV_HALO, :] = jnp.zeros((CONV_HALO, CONV_WIDTH), F32)

    zbuf_ref[CONV_HALO:CONV_HALO + tm, :] = z
    cw = convw_ref[...]
    zc = (cw[0:1, :] * zbuf_ref[CONV_HALO - 2:CONV_HALO - 2 + tm, :]
          + cw[1:2, :] * zbuf_ref[CONV_HALO - 1:CONV_HALO - 1 + tm, :]
          + cw[2:3, :] * z)
    zbuf_ref[0:CONV_HALO, :] = z[tm - CONV_HALO:, :]
    ya = _mm((proj(OFF_BG, CONV_WIDTH) * zc).astype(BF16), wco_ref[...])
    g_conv = _sigmoid(proj(OFF_GC, D_MODEL) + bgate_ref[:, 0:D_MODEL])
    gy_ref[0] = (g_conv * ya).astype(gy_ref.dtype)
    ga_ref[0] = _sigmoid(proj(OFF_GA, D_MODEL) + bgate_ref[:, D_MODEL:]).astype(ga_ref.dtype)

    lane = lax.broadcasted_iota(jnp.int32, (tm, 128), 1)
    lo_half = (lane % DA_HEAD_DIM) < (DA_HEAD_DIM // 2)
    q_ref[0] = _rope(proj(OFF_Q, QK_WIDTH), cosq_ref[...], sinq_ref[...], lo_half).astype(BF16)
    k_ref[0] = _rope(proj(OFF_K, QK_WIDTH), cosk_ref[...], sink_ref[...], lo_half).astype(BF16)
    v_ref[0] = proj(OFF_V, V_WIDTH).astype(BF16)


def _in_proj(h3d, g, w_in, b_gate, conv_w, w_conv_out, cosq, sinq, cosk, sink):
    b, s, _ = h3d.shape
    tm = TM_PROJ
    tok = lambda bi, si: (bi, si, 0)
    tab = lambda bi, si: (si, 0)
    bsd = lambda width, dt: jax.ShapeDtypeStruct((b, s, width), dt)
    return pl.pallas_call(
        _in_proj_kernel,
        grid=(b, s // tm),
        in_specs=[
            pl.BlockSpec((1, tm, D_MODEL), tok),
            _const_spec((1, D_MODEL)),
            _const_spec((D_MODEL, IN_WIDTH)),
            _const_spec((1, 2 * D_MODEL)),
            _const_spec((CONV_K, CONV_WIDTH)),
            _const_spec((CONV_WIDTH, D_MODEL)),
            pl.BlockSpec((tm, 128), tab),
            pl.BlockSpec((tm, 128), tab),
            pl.BlockSpec((tm, 128), tab),
            pl.BlockSpec((tm, 128), tab),
        ],
        out_specs=[
            pl.BlockSpec((1, tm, D_MODEL), tok),
            pl.BlockSpec((1, tm, D_MODEL), tok),
            pl.BlockSpec((1, tm, QK_WIDTH), tok),
            pl.BlockSpec((1, tm, QK_WIDTH), tok),
            pl.BlockSpec((1, tm, V_WIDTH), tok),
        ],
        out_shape=[bsd(D_MODEL, BF16), bsd(D_MODEL, BF16), bsd(QK_WIDTH, BF16),
                   bsd(QK_WIDTH, BF16), bsd(V_WIDTH, BF16)],
        scratch_shapes=[pltpu.VMEM((CONV_HALO + tm, CONV_WIDTH), F32)],
        name="in_proj",
        compiler_params=pltpu.CompilerParams(
            dimension_semantics=("arbitrary", "arbitrary"),
            vmem_limit_bytes=V7X_VMEM_LIMIT_BYTES),
    )(h3d, g, w_in, b_gate, conv_w, w_conv_out, cosq, sinq, cosk, sink)


def _attn_kernel(q_ref, k_ref, v_ref, lamv_ref, sg_ref, o_ref):
    qi = pl.program_id(2)
    tq = q_ref.shape[1]

    q = q_ref[0]
    lane = lax.broadcasted_iota(jnp.int32, q.shape, 1)
    zero = jnp.zeros_like(q)
    qq = jnp.concatenate([jnp.where(lane < DA_HEAD_DIM, q, zero),
                          jnp.where(lane >= DA_HEAD_DIM, q, zero)], axis=0)

    def step(start, carry, masked):
        m, l, acc = carry
        kt = k_ref[0, pl.ds(start, TK), :]
        vt = v_ref[0, pl.ds(start, TK), :]
        s = lax.dot_general(qq, kt, (((1,), (1,)), ((), ())), preferred_element_type=F32)
        if masked:
            row = lax.broadcasted_iota(jnp.int32, (tq, TK), 0) // CHUNK
            col = lax.broadcasted_iota(jnp.int32, (tq, TK), 1) // CHUNK
            keep = col <= row
            keep = jnp.concatenate([keep, keep], axis=0)
            s = jnp.where(keep, s, -jnp.inf)
        m_new = jnp.maximum(m, jnp.max(s, axis=1, keepdims=True))
        p = jnp.exp(s - m_new)
        alpha = jnp.exp(m - m_new)
        l = alpha * l + jnp.sum(p, axis=1, keepdims=True)
        acc = alpha * acc + _mm(p.astype(BF16), vt)
        return m_new, l, acc

    init = (jnp.full((2 * tq, 1), -jnp.inf, F32), jnp.zeros((2 * tq, 1), F32),
            jnp.zeros((2 * tq, DA_V_DIM), F32))
    n_full = qi * (tq // TK)
    carry = lax.fori_loop(
        0, n_full, lambda i, c: step(pl.multiple_of(i * TK, TK), c, False), init)
    _, l, acc = step(pl.multiple_of(qi * tq, tq), carry, True)

    lv = lamv_ref[...]
    lam = (jnp.exp(jnp.sum(lv[0:1] * lv[1:2], axis=-1, keepdims=True))
           - jnp.exp(jnp.sum(lv[2:3] * lv[3:4], axis=-1, keepdims=True)) + LAMBDA_INIT)
    o = acc / l
    o = o[:tq] - lam * o[tq:]
    o = _rms(o, sg_ref[...]) * (1.0 - LAMBDA_INIT)
    o_ref[0] = o.astype(o_ref.dtype)


def _diff_attn(q, k, v, lamv, subln_g):
    b, s, _ = q.shape
    return pl.pallas_call(
        _attn_kernel,
        grid=(b, DA_HEADS, s // TQ),
        in_specs=[
            pl.BlockSpec((1, TQ, 128), lambda bi, hi, qi: (bi, qi, hi)),
            pl.BlockSpec((1, s, 128), lambda bi, hi, qi: (bi, 0, hi)),
            pl.BlockSpec((1, s, 128), lambda bi, hi, qi: (bi, 0, hi)),
            pl.BlockSpec((4, DA_HEAD_DIM), lambda bi, hi, qi: (0, 0)),
            pl.BlockSpec((1, DA_V_DIM), lambda bi, hi, qi: (0, 0)),
        ],
        out_specs=pl.BlockSpec((1, TQ, 128), lambda bi, hi, qi: (bi, qi, hi)),
        out_shape=jax.ShapeDtypeStruct((b, s, V_WIDTH), BF16),
        name="diff_attn",
        compiler_params=pltpu.CompilerParams(
            dimension_semantics=("arbitrary", "arbitrary", "arbitrary"),
            vmem_limit_bytes=V7X_VMEM_LIMIT_BYTES),
    )(q, k, v, lamv, subln_g)


def _merge_kernel(h_ref, o_ref, gy_ref, ga_ref, wao_ref, wo_ref, out_ref):
    y_attn = _mm(o_ref[...], wao_ref[...])
    mix = gy_ref[...].astype(F32) + ga_ref[...].astype(F32) * y_attn
    out_ref[...] = h_ref[...] + _mm(mix.astype(BF16), wo_ref[...])


def _merge(h2d, o2d, gy2d, ga2d, wao, wo):
    t = h2d.shape[0]
    tm = TM_MERGE
    tok = lambda i: (i, 0)
    return pl.pallas_call(
        _merge_kernel,
        grid=(t // tm,),
        in_specs=[
            pl.BlockSpec((tm, D_MODEL), tok),
            pl.BlockSpec((tm, V_WIDTH), tok),
            pl.BlockSpec((tm, D_MODEL), tok),
            pl.BlockSpec((tm, D_MODEL), tok),
            _const_spec((V_WIDTH, D_MODEL)),
            _const_spec((D_MODEL, D_MODEL)),
        ],
        out_specs=pl.BlockSpec((tm, D_MODEL), tok),
        out_shape=jax.ShapeDtypeStruct((t, D_MODEL), F32),
        name="merge",
        compiler_params=pltpu.CompilerParams(
            dimension_semantics=("arbitrary",), vmem_limit_bytes=V7X_VMEM_LIMIT_BYTES),
    )(h2d, o2d, gy2d, ga2d, wao, wo)


def _rope_tables(seq):
    half = DA_HEAD_DIM // 2
    inv_freq = 1.0 / (ROPE_THETA ** (jnp.arange(0, DA_HEAD_DIM, 2, dtype=F32) / DA_HEAD_DIM))
    ang = jnp.arange(seq, dtype=F32)[:, None] * inv_freq[None, :]
    cos, sin = jnp.cos(ang), jnp.sin(ang)
    cos128 = jnp.tile(cos, (1, 128 // half))
    sin128 = jnp.tile(jnp.concatenate([-sin, sin], axis=1), (1, 128 // DA_HEAD_DIM))
    return cos128, sin128


def kernel(x, norm_ffn1, ffn1_gate, ffn1_up, ffn1_down, norm_mix, w_in, b_gate, conv_w,
           w_conv_out, lambda_q1, lambda_k1, lambda_q2, lambda_k2, subln_g, w_attn_out, w_o,
           norm_ffn2, ffn2_gate, ffn2_up, ffn2_down, norm_final):
    bsz, seq, d = x.shape
    assert d == D_MODEL and norm_ffn1.shape[0] == 1
    assert seq % TQ == 0 and seq % TM_PROJ == 0 and (bsz * seq) % TM_FFN == 0
    t = bsz * seq
    l = 0
    bf = lambda a: a.astype(BF16)
    row = lambda a: a.reshape(1, -1)

    cos128, sin128 = _rope_tables(seq)
    scale = 1.0 / math.sqrt(DA_HEAD_DIM)
    lamv = jnp.stack([lambda_q1[l], lambda_k1[l], lambda_q2[l], lambda_k2[l]]).astype(F32)

    h = _ffn(x.reshape(t, d), row(norm_ffn1[l]), bf(ffn1_gate[l]), bf(ffn1_up[l]),
             bf(ffn1_down[l]), row(norm_final), final_norm=False)
    gy, ga, q, k, v = _in_proj(
        h.reshape(bsz, seq, d), row(norm_mix[l]), bf(w_in[l]), row(b_gate[l]), conv_w[l],
        bf(w_conv_out[l]), cos128 * scale, sin128 * scale, cos128, sin128)
    o = _diff_attn(q, k, v, lamv, row(subln_g[l]))
    h2 = _merge(h, o.reshape(t, V_WIDTH), gy.reshape(t, d), ga.reshape(t, d),
                bf(w_attn_out[l]), bf(w_o[l]))
    out = _ffn(h2, row(norm_ffn2[l]), bf(ffn2_gate[l]), bf(ffn2_up[l]), bf(ffn2_down[l]),
               row(norm_final), final_norm=True)
    return out.reshape(bsz, seq, d)
```

```python
import functools
import math

import jax
import jax.numpy as jnp
from jax import lax
from jax.experimental import pallas as pl
from jax.experimental.pallas import tpu as pltpu

D_MODEL = 1024
CHUNK = 64
CONV_WIDTH = 512
CONV_K = 3
DA_HEADS = 4
DA_HEAD_DIM = 64
DA_V_DIM = 128
QK_WIDTH = 512
V_WIDTH = 512
D_FF = 2816
ROPE_THETA = 10000.0
NORM_EPS = 1e-6
LAMBDA_INIT = 0.8 - 0.6 * math.exp(-0.3 * 0)

OFF_XC, OFF_BG, OFF_CG, OFF_Q, OFF_K, OFF_V, OFF_GC, OFF_GA = (
    0, 512, 1024, 1536, 2048, 2560, 3072, 4096)
IN_WIDTH = 5120

V7X_VMEM_LIMIT_BYTES = 56 * 1024 * 1024

TM_FFN = 512
TM_PROJ = 512
TM_MERGE = 512
TQ = 512
TK = 512
CONV_HALO = 8

BF16 = jnp.bfloat16
F32 = jnp.float32


def _const_spec(shape):
    return pl.BlockSpec(shape, lambda *_: (0,) * len(shape), pipeline_mode=pl.Buffered(1))


def _rms(x, g):
    return x * lax.rsqrt(jnp.mean(x * x, axis=-1, keepdims=True) + NORM_EPS) * g


def _sigmoid(x):
    return 1.0 / (1.0 + jnp.exp(-x))


def _mm(a, b):
    return jnp.dot(a, b, preferred_element_type=F32)


def _ffn_kernel(x_ref, g_ref, wg_ref, wu_ref, wd_ref, gf_ref, o_ref, *, final_norm):
    x = x_ref[...]
    xn = _rms(x, g_ref[...]).astype(BF16)
    gate = _mm(xn, wg_ref[...])
    up = _mm(xn, wu_ref[...])
    a = (gate * _sigmoid(gate) * up).astype(BF16)
    y = x + 0.5 * _mm(a, wd_ref[...])
    if final_norm:
        y = _rms(y, gf_ref[...])
    o_ref[...] = y


def _ffn(x2d, g, wg, wu, wd, gf, *, final_norm):
    t = x2d.shape[0]
    tok = lambda i: (i, 0)
    return pl.pallas_call(
        functools.partial(_ffn_kernel, final_norm=final_norm),
        grid=(t // TM_FFN,),
        in_specs=[
            pl.BlockSpec((TM_FFN, D_MODEL), tok),
            _const_spec((1, D_MODEL)),
            _const_spec((D_MODEL, D_FF)),
            _const_spec((D_MODEL, D_FF)),
            _const_spec((D_FF, D_MODEL)),
            _const_spec((1, D_MODEL)),
        ],
        out_specs=pl.BlockSpec((TM_FFN, D_MODEL), tok),
        out_shape=jax.ShapeDtypeStruct((t, D_MODEL), F32),
        name="ffn_final" if final_norm else "ffn",
        compiler_params=pltpu.CompilerParams(
            dimension_semantics=("arbitrary",), vmem_limit_bytes=V7X_VMEM_LIMIT_BYTES),
    )(x2d, g, wg, wu, wd, gf)


def _rope(x, cos, sin_signed, lo_half):
    outs = []
    for c in range(x.shape[1] // 128):
        xc = x[:, c * 128:(c + 1) * 128]
        rot = jnp.where(lo_half, pltpu.roll(xc, 96, 1), pltpu.roll(xc, 32, 1))
        outs.append(xc * cos + rot * sin_signed)
    return jnp.concatenate(outs, axis=1)


def _in_proj_kernel(h_ref, g_ref, w_ref, bgate_ref, convw_ref, wco_ref, cosq_ref, sinq_ref,
                    cosk_ref, sink_ref, gy_ref, ga_ref, q_ref, k_ref, v_ref, zbuf_ref):
    si = pl.program_id(1)
    tm = h_ref.shape[1]
    u = _rms(h_ref[0], g_ref[...]).astype(BF16)

    def proj(off, width):
        return _mm(u, w_ref[:, off:off + width])

    z = proj(OFF_CG, CONV_WIDTH) * proj(OFF_XC, CONV_WIDTH)

    @pl.when(si == 0)
    def _():
        zbuf_ref[0:CONV_HALO, :] = jnp.zeros((CONV_HALO, CONV_WIDTH), F32)

    zbuf_ref[CONV_HALO:CONV_HALO + tm, :] = z
    cw = convw_ref[...]
    zc = (cw[0:1, :] * zbuf_ref[CONV_HALO - 2:CONV_HALO - 2 + tm, :]
          + cw[1:2, :] * zbuf_ref[CONV_HALO - 1:CONV_HALO - 1 + tm, :]
          + cw[2:3, :] * z)
    zbuf_ref[0:CONV_HALO, :] = z[tm - CONV_HALO:, :]
    ya = _mm((proj(OFF_BG, CONV_WIDTH) * zc).astype(BF16), wco_ref[...])
    g_conv = _sigmoid(proj(OFF_GC, D_MODEL) + bgate_ref[:, 0:D_MODEL])
    gy_ref[0] = (g_conv * ya).astype(gy_ref.dtype)
    ga_ref[0] = _sigmoid(proj(OFF_GA, D_MODEL) + bgate_ref[:, D_MODEL:]).astype(ga_ref.dtype)

    lane = lax.broadcasted_iota(jnp.int32, (tm, 128), 1)
    lo_half = (lane % DA_HEAD_DIM) < (DA_HEAD_DIM // 2)
    q_ref[0] = _rope(proj(OFF_Q, QK_WIDTH), cosq_ref[...], sinq_ref[...], lo_half).astype(BF16)
    k_ref[0] = _rope(proj(OFF_K, QK_WIDTH), cosk_ref[...], sink_ref[...], lo_half).astype(BF16)
    v_ref[0] = proj(OFF_V, V_WIDTH).astype(BF16)


def _in_proj(h3d, g, w_in, b_gate, conv_w, w_conv_out, cosq, sinq, cosk, sink):
    b, s, _ = h3d.shape
    tm = TM_PROJ
    tok = lambda bi, si: (bi, si, 0)
    tab = lambda bi, si: (si, 0)
    bsd = lambda width, dt: jax.ShapeDtypeStruct((b, s, width), dt)
    return pl.pallas_call(
        _in_proj_kernel,
        grid=(b, s // tm),
        in_specs=[
            pl.BlockSpec((1, tm, D_MODEL), tok),
            _const_spec((1, D_MODEL)),
            _const_spec((D_MODEL, IN_WIDTH)),
            _const_spec((1, 2 * D_MODEL)),
            _const_spec((CONV_K, CONV_WIDTH)),
            _const_spec((CONV_WIDTH, D_MODEL)),
            pl.BlockSpec((tm, 128), tab),
            pl.BlockSpec((tm, 128), tab),
            pl.BlockSpec((tm, 128), tab),
            pl.BlockSpec((tm, 128), tab),
        ],
        out_specs=[
            pl.BlockSpec((1, tm, D_MODEL), tok),
            pl.BlockSpec((1, tm, D_MODEL), tok),
            pl.BlockSpec((1, tm, QK_WIDTH), tok),
            pl.BlockSpec((1, tm, QK_WIDTH), tok),
            pl.BlockSpec((1, tm, V_WIDTH), tok),
        ],
        out_shape=[bsd(D_MODEL, BF16), bsd(D_MODEL, BF16), bsd(QK_WIDTH, BF16),
                   bsd(QK_WIDTH, BF16), bsd(V_WIDTH, BF16)],
        scratch_shapes=[pltpu.VMEM((CONV_HALO + tm, CONV_WIDTH), F32)],
        name="in_proj",
        compiler_params=pltpu.CompilerParams(
            dimension_semantics=("arbitrary", "arbitrary"),
            vmem_limit_bytes=V7X_VMEM_LIMIT_BYTES),
    )(h3d, g, w_in, b_gate, conv_w, w_conv_out, cosq, sinq, cosk, sink)


def _attn_kernel(q_ref, k_ref, v_ref, bias_ref, lamv_ref, sg_ref, o_ref, s_ref, acc_ref, m_ref):
    qi = pl.program_id(2)
    tq = q_ref.shape[1]

    q = q_ref[0]
    lane = lax.broadcasted_iota(jnp.int32, q.shape, 1)
    zero = jnp.zeros_like(q)
    qq = jnp.concatenate([jnp.where(lane < DA_HEAD_DIM, q, zero),
                          jnp.where(lane >= DA_HEAD_DIM, q, zero)], axis=0)
    ones = jnp.ones((TK, DA_V_DIM), BF16)

    def scores(start):
        kt = k_ref[0, pl.ds(start, TK), :]
        return lax.dot_general(qq, kt, (((1,), (1,)), ((), ())), preferred_element_type=F32)

    def accumulate(tile, slot):
        start = pl.multiple_of(tile * TK, TK)
        vt = jnp.concatenate([v_ref[0, pl.ds(start, TK), :], ones], axis=1)
        s = s_ref[slot]
        m = m_ref[:, 0:1]
        m_new = jnp.maximum(m, jnp.max(s, axis=1, keepdims=True))
        p = jnp.exp2((s - m_new).astype(BF16))
        acc_ref[...] = jnp.exp2(m - m_new) * acc_ref[...] + _mm(p, vt)
        m_ref[...] = jnp.broadcast_to(m_new, m_ref.shape)

    def substep(k, src, dst):
        s_ref[dst] = scores(pl.multiple_of(k * TK, TK))
        accumulate(jnp.where(k == 0, qi, k - 1), src)

    bias = bias_ref[...]
    s_ref[0] = scores(pl.multiple_of(qi * tq, tq)) + jnp.concatenate([bias, bias], axis=0)
    acc_ref[...] = jnp.zeros_like(acc_ref)
    m_ref[...] = jnp.full(m_ref.shape, -jnp.inf, F32)
    n_full = qi * (tq // TK)

    def pair(j, _):
        substep(2 * j, 0, 1)
        substep(2 * j + 1, 1, 0)
        return 0

    lax.fori_loop(0, n_full // 2, pair, 0)
    last = jnp.where(n_full == 0, qi, n_full - 1)

    @pl.when(n_full % 2 == 1)
    def _():
        substep(n_full - 1, 0, 1)
        accumulate(last, 1)

    @pl.when(n_full % 2 == 0)
    def _():
        accumulate(last, 0)

    acc = acc_ref[...]

    lv = lamv_ref[...]
    lam = (jnp.exp(jnp.sum(lv[0:1] * lv[1:2], axis=-1, keepdims=True))
           - jnp.exp(jnp.sum(lv[2:3] * lv[3:4], axis=-1, keepdims=True)) + LAMBDA_INIT)
    o = acc[:, :DA_V_DIM] / acc[:, DA_V_DIM:]
    o = o[:tq] - lam * o[tq:]
    o = _rms(o, sg_ref[...]) * (1.0 - LAMBDA_INIT)
    o_ref[0] = o.astype(o_ref.dtype)


def _diff_attn(q, k, v, bias, lamv, subln_g):
    b, s, _ = q.shape
    return pl.pallas_call(
        _attn_kernel,
        grid=(b, DA_HEADS, s // TQ),
        in_specs=[
            pl.BlockSpec((1, TQ, 128), lambda bi, hi, qi: (bi, qi, hi)),
            pl.BlockSpec((1, s, 128), lambda bi, hi, qi: (bi, 0, hi)),
            pl.BlockSpec((1, s, 128), lambda bi, hi, qi: (bi, 0, hi)),
            _const_spec((TQ, TK)),
            _const_spec((4, DA_HEAD_DIM)),
            _const_spec((1, DA_V_DIM)),
        ],
        out_specs=pl.BlockSpec((1, TQ, 128), lambda bi, hi, qi: (bi, qi, hi)),
        out_shape=jax.ShapeDtypeStruct((b, s, V_WIDTH), BF16),
        scratch_shapes=[pltpu.VMEM((2, 2 * TQ, TK), F32),
                        pltpu.VMEM((2 * TQ, 2 * DA_V_DIM), F32),
                        pltpu.VMEM((2 * TQ, 128), F32)],
        name="diff_attn",
        compiler_params=pltpu.CompilerParams(
            dimension_semantics=("arbitrary", "arbitrary", "arbitrary"),
            vmem_limit_bytes=V7X_VMEM_LIMIT_BYTES),
    )(q, k, v, bias, lamv, subln_g)


def _merge_kernel(h_ref, o_ref, gy_ref, ga_ref, wao_ref, wo_ref, out_ref):
    y_attn = _mm(o_ref[...], wao_ref[...])
    mix = gy_ref[...].astype(F32) + ga_ref[...].astype(F32) * y_attn
    out_ref[...] = h_ref[...] + _mm(mix.astype(BF16), wo_ref[...])


def _merge(h2d, o2d, gy2d, ga2d, wao, wo):
    t = h2d.shape[0]
    tm = TM_MERGE
    tok = lambda i: (i, 0)
    return pl.pallas_call(
        _merge_kernel,
        grid=(t // tm,),
        in_specs=[
            pl.BlockSpec((tm, D_MODEL), tok),
            pl.BlockSpec((tm, V_WIDTH), tok),
            pl.BlockSpec((tm, D_MODEL), tok),
            pl.BlockSpec((tm, D_MODEL), tok),
            _const_spec((V_WIDTH, D_MODEL)),
            _const_spec((D_MODEL, D_MODEL)),
        ],
        out_specs=pl.BlockSpec((tm, D_MODEL), tok),
        out_shape=jax.ShapeDtypeStruct((t, D_MODEL), F32),
        name="merge",
        compiler_params=pltpu.CompilerParams(
            dimension_semantics=("arbitrary",), vmem_limit_bytes=V7X_VMEM_LIMIT_BYTES),
    )(h2d, o2d, gy2d, ga2d, wao, wo)


def _rope_tables(seq):
    half = DA_HEAD_DIM // 2
    inv_freq = 1.0 / (ROPE_THETA ** (jnp.arange(0, DA_HEAD_DIM, 2, dtype=F32) / DA_HEAD_DIM))
    ang = jnp.arange(seq, dtype=F32)[:, None] * inv_freq[None, :]
    cos, sin = jnp.cos(ang), jnp.sin(ang)
    cos128 = jnp.tile(cos, (1, 128 // half))
    sin128 = jnp.tile(jnp.concatenate([-sin, sin], axis=1), (1, 128 // DA_HEAD_DIM))
    return cos128, sin128


def kernel(x, norm_ffn1, ffn1_gate, ffn1_up, ffn1_down, norm_mix, w_in, b_gate, conv_w,
           w_conv_out, lambda_q1, lambda_k1, lambda_q2, lambda_k2, subln_g, w_attn_out, w_o,
           norm_ffn2, ffn2_gate, ffn2_up, ffn2_down, norm_final):
    bsz, seq, d = x.shape
    assert d == D_MODEL and norm_ffn1.shape[0] == 1
    assert seq % TQ == 0 and seq % TM_PROJ == 0 and (bsz * seq) % TM_FFN == 0
    assert TQ == TK and TQ % CHUNK == 0
    t = bsz * seq
    l = 0
    bf = lambda a: a.astype(BF16)
    row = lambda a: a.reshape(1, -1)

    cos128, sin128 = _rope_tables(seq)
    scale = math.log2(math.e) / math.sqrt(DA_HEAD_DIM)
    pos_chunk = jnp.arange(TQ, dtype=jnp.int32) // CHUNK
    bias = jnp.where(pos_chunk[None, :] <= pos_chunk[:, None], 0.0, -jnp.inf).astype(F32)
    lamv = jnp.stack([lambda_q1[l], lambda_k1[l], lambda_q2[l], lambda_k2[l]]).astype(F32)

    h = _ffn(x.reshape(t, d), row(norm_ffn1[l]), bf(ffn1_gate[l]), bf(ffn1_up[l]),
             bf(ffn1_down[l]), row(norm_final), final_norm=False)
    gy, ga, q, k, v = _in_proj(
        h.reshape(bsz, seq, d), row(norm_mix[l]), bf(w_in[l]), row(b_gate[l]), conv_w[l],
        bf(w_conv_out[l]), cos128 * scale, sin128 * scale, cos128, sin128)
    o = _diff_attn(q, k, v, bias, lamv, row(subln_g[l]))
    h2 = _merge(h, o.reshape(t, V_WIDTH), gy.reshape(t, d), ga.reshape(t, d),
                bf(w_attn_out[l]), bf(w_o[l]))
    out = _ffn(h2, row(norm_ffn2[l]), bf(ffn2_gate[l]), bf(ffn2_up[l]), bf(ffn2_down[l]),
               row(norm_final), final_norm=True)
    return out.reshape(bsz, seq, d)
```

```python
import functools
import math

import jax
import jax.numpy as jnp
from jax import lax
from jax.experimental import pallas as pl
from jax.experimental.pallas import tpu as pltpu

D_MODEL = 1024
CHUNK = 64
CONV_WIDTH = 512
CONV_K = 3
DA_HEADS = 4
DA_HEAD_DIM = 64
DA_V_DIM = 128
QK_WIDTH = 512
V_WIDTH = 512
D_FF = 2816
ROPE_THETA = 10000.0
NORM_EPS = 1e-6
LAMBDA_INIT = 0.8 - 0.6 * math.exp(-0.3 * 0)

OFF_XC, OFF_BG, OFF_CG, OFF_Q, OFF_K, OFF_V, OFF_GC, OFF_GA = (
    0, 512, 1024, 1536, 2048, 2560, 3072, 4096)
IN_WIDTH = 5120

V7X_VMEM_LIMIT_BYTES = 56 * 1024 * 1024
BF16_SUBLANES = 16

TM_FFN = 512
TM_PROJ = 512
TM_MERGE = 512
TQ = 512
TK = 512
HEADS_PER_STEP = 2
CONV_HALO = 8

BF16 = jnp.bfloat16
F32 = jnp.float32


def _const_spec(shape):
    return pl.BlockSpec(shape, lambda *_: (0,) * len(shape), pipeline_mode=pl.Buffered(1))


def _rms(x, g):
    return x * lax.rsqrt(jnp.mean(x * x, axis=-1, keepdims=True) + NORM_EPS) * g


def _sigmoid(x):
    return 1.0 / (1.0 + jnp.exp(-x))


def _mm(a, b):
    return jnp.dot(a, b, preferred_element_type=F32)


def _mm_nt(a, b):
    return lax.dot_general(a, b, (((1,), (1,)), ((), ())), preferred_element_type=F32)


def _ffn_kernel(x_ref, g_ref, wg_ref, wu_ref, wd_ref, gf_ref, o_ref, *, final_norm):
    x = x_ref[...]
    xn = _rms(x, g_ref[...]).astype(BF16)
    gate = _mm(xn, wg_ref[...])
    up = _mm(xn, wu_ref[...])
    a = (gate * _sigmoid(gate) * up).astype(BF16)
    y = x + 0.5 * _mm(a, wd_ref[...])
    if final_norm:
        y = _rms(y, gf_ref[...])
    o_ref[...] = y


def _ffn(x2d, g, wg, wu, wd, gf, *, final_norm):
    t = x2d.shape[0]
    tok = lambda i: (i, 0)
    return pl.pallas_call(
        functools.partial(_ffn_kernel, final_norm=final_norm),
        grid=(t // TM_FFN,),
        in_specs=[
            pl.BlockSpec((TM_FFN, D_MODEL), tok),
            _const_spec((1, D_MODEL)),
            _const_spec((D_MODEL, D_FF)),
            _const_spec((D_MODEL, D_FF)),
            _const_spec((D_FF, D_MODEL)),
            _const_spec((1, D_MODEL)),
        ],
        out_specs=pl.BlockSpec((TM_FFN, D_MODEL), tok),
        out_shape=jax.ShapeDtypeStruct((t, D_MODEL), F32),
        name="ffn_final" if final_norm else "ffn",
        compiler_params=pltpu.CompilerParams(
            dimension_semantics=("arbitrary",), vmem_limit_bytes=V7X_VMEM_LIMIT_BYTES),
    )(x2d, g, wg, wu, wd, gf)


def _rope(x, cos, sin_signed, lo_half):
    outs = []
    for c in range(x.shape[1] // 128):
        xc = x[:, c * 128:(c + 1) * 128]
        rot = jnp.where(lo_half, pltpu.roll(xc, 96, 1), pltpu.roll(xc, 32, 1))
        outs.append(xc * cos + rot * sin_signed)
    return jnp.concatenate(outs, axis=1)


def _in_proj_kernel(h_ref, g_ref, w_ref, bgate_ref, convw_ref, wco_ref, cosq_ref, sinq_ref,
                    cosk_ref, sink_ref, gy_ref, ga_ref, q_ref, k_ref, vt_ref, zbuf_ref):
    si = pl.program_id(1)
    tm = h_ref.shape[1]
    u = _rms(h_ref[0], g_ref[...]).astype(BF16)

    def proj(off, width):
        return _mm(u, w_ref[:, off:off + width])

    z = proj(OFF_CG, CONV_WIDTH) * proj(OFF_XC, CONV_WIDTH)

    @pl.when(si == 0)
    def _():
        zbuf_ref[0:CONV_HALO, :] = jnp.zeros((CONV_HALO, CONV_WIDTH), F32)

    zbuf_ref[CONV_HALO:CONV_HALO + tm, :] = z
    cw = convw_ref[...]
    zc = (cw[0:1, :] * zbuf_ref[CONV_HALO - 2:CONV_HALO - 2 + tm, :]
          + cw[1:2, :] * zbuf_ref[CONV_HALO - 1:CONV_HALO - 1 + tm, :]
          + cw[2:3, :] * z)
    zbuf_ref[0:CONV_HALO, :] = z[tm - CONV_HALO:, :]
    ya = _mm((proj(OFF_BG, CONV_WIDTH) * zc).astype(BF16), wco_ref[...])
    g_conv = _sigmoid(proj(OFF_GC, D_MODEL) + bgate_ref[:, 0:D_MODEL])
    gy_ref[0] = (g_conv * ya).astype(gy_ref.dtype)
    ga_ref[0] = _sigmoid(proj(OFF_GA, D_MODEL) + bgate_ref[:, D_MODEL:]).astype(ga_ref.dtype)

    lane = lax.broadcasted_iota(jnp.int32, (tm, 128), 1)
    lo_half = (lane % DA_HEAD_DIM) < (DA_HEAD_DIM // 2)
    q_ref[0] = _rope(proj(OFF_Q, QK_WIDTH), cosq_ref[...], sinq_ref[...], lo_half).astype(BF16)
    k_ref[0] = _rope(proj(OFF_K, QK_WIDTH), cosk_ref[...], sink_ref[...], lo_half).astype(BF16)
    vt_ref[0] = proj(OFF_V, V_WIDTH).T.astype(BF16)


def _in_proj(h3d, g, w_in, b_gate, conv_w, w_conv_out, cosq, sinq, cosk, sink):
    b, s, _ = h3d.shape
    tm = TM_PROJ
    tok = lambda bi, si: (bi, si, 0)
    tab = lambda bi, si: (si, 0)
    bsd = lambda width, dt: jax.ShapeDtypeStruct((b, s, width), dt)
    return pl.pallas_call(
        _in_proj_kernel,
        grid=(b, s // tm),
        in_specs=[
            pl.BlockSpec((1, tm, D_MODEL), tok),
            _const_spec((1, D_MODEL)),
            _const_spec((D_MODEL, IN_WIDTH)),
            _const_spec((1, 2 * D_MODEL)),
            _const_spec((CONV_K, CONV_WIDTH)),
            _const_spec((CONV_WIDTH, D_MODEL)),
            pl.BlockSpec((tm, 128), tab),
            pl.BlockSpec((tm, 128), tab),
            pl.BlockSpec((tm, 128), tab),
            pl.BlockSpec((tm, 128), tab),
        ],
        out_specs=[
            pl.BlockSpec((1, tm, D_MODEL), tok),
            pl.BlockSpec((1, tm, D_MODEL), tok),
            pl.BlockSpec((1, tm, QK_WIDTH), tok),
            pl.BlockSpec((1, tm, QK_WIDTH), tok),
            pl.BlockSpec((1, V_WIDTH, tm), lambda bi, si: (bi, 0, si)),
        ],
        out_shape=[bsd(D_MODEL, BF16), bsd(D_MODEL, BF16), bsd(QK_WIDTH, BF16),
                   bsd(QK_WIDTH, BF16), jax.ShapeDtypeStruct((b, V_WIDTH, s), BF16)],
        scratch_shapes=[pltpu.VMEM((CONV_HALO + tm, CONV_WIDTH), F32)],
        name="in_proj",
        compiler_params=pltpu.CompilerParams(
            dimension_semantics=("arbitrary", "arbitrary"),
            vmem_limit_bytes=V7X_VMEM_LIMIT_BYTES),
    )(h3d, g, w_in, b_gate, conv_w, w_conv_out, cosq, sinq, cosk, sink)


def _attn_kernel(q_ref, k_ref, vt_ref, bias_ref, lamv_ref, sg_ref, o_ref,
                 acc_ref, sa_ref, sb_ref):
    qi = pl.program_id(2)
    tq = q_ref.shape[1]
    heads = range(HEADS_PER_STEP)
    hcols = lambda hh: slice(hh * 128, (hh + 1) * 128)

    lane = lax.broadcasted_iota(jnp.int32, (tq, 128), 1)
    zero = jnp.zeros((tq, 128), BF16)
    qq = []
    for hh in heads:
        q = q_ref[0, :, hcols(hh)]
        qq.append(jnp.concatenate([jnp.where(lane < DA_HEAD_DIM, q, zero),
                                   jnp.where(lane >= DA_HEAD_DIM, q, zero)], axis=0))
    ones = jnp.ones((BF16_SUBLANES, TK), BF16)

    def scores(hh, tile):
        kt = k_ref[0, pl.ds(pl.multiple_of(tile * TK, TK), TK), hcols(hh)]
        return _mm_nt(kt, qq[hh])

    def accumulate(hh, tile, s, m):
        start = pl.multiple_of(tile * TK, TK)
        vt = jnp.concatenate([vt_ref[0, hcols(hh), pl.ds(start, TK)], ones], axis=0)
        m_new = jnp.maximum(m, jnp.max(s, axis=0, keepdims=True))
        p = jnp.exp2((s - m_new).astype(BF16))
        acc_ref[hh] = jnp.exp2(m - m_new) * acc_ref[hh] + _mm(vt, p)
        return m_new

    acc_ref[...] = jnp.zeros_like(acc_ref)
    sa_ref[...] = scores(0, 0) + bias_ref[(qi == 0).astype(jnp.int32)]

    def body(j, ms):
        ma, mb = ms
        sb_ref[...] = scores(1, j)
        ma = accumulate(0, j, sa_ref[...], ma)
        sa_ref[...] = scores(0, j + 1) + bias_ref[(j + 1 == qi).astype(jnp.int32)]
        mb = accumulate(1, j, sb_ref[...], mb)
        return ma, mb

    m0 = jnp.full((1, 2 * tq), -jnp.inf, F32)
    ma, mb = lax.fori_loop(0, qi, body, (m0, m0))
    sb_ref[...] = scores(1, qi) + bias_ref[1]
    accumulate(0, qi, sa_ref[...], ma)
    accumulate(1, qi, sb_ref[...], mb)

    lv = lamv_ref[...]
    lam = (jnp.exp(jnp.sum(lv[0:1] * lv[1:2], axis=-1, keepdims=True))
           - jnp.exp(jnp.sum(lv[2:3] * lv[3:4], axis=-1, keepdims=True)) + LAMBDA_INIT)
    for hh in heads:
        o = acc_ref[hh, 0:DA_V_DIM, :] / acc_ref[hh, DA_V_DIM:DA_V_DIM + 1, :]
        o = o[:, :tq] - lam * o[:, tq:]
        o = o * lax.rsqrt(jnp.mean(o * o, axis=0, keepdims=True) + NORM_EPS)
        o = o * (sg_ref[...] * (1.0 - LAMBDA_INIT))
        o_ref[0, :, hcols(hh)] = o.T.astype(o_ref.dtype)


def _diff_attn(q, k, vt, bias, lamv, subln_g_col):
    b, s, _ = q.shape
    hw = 128 * HEADS_PER_STEP
    return pl.pallas_call(
        _attn_kernel,
        grid=(b, DA_HEADS // HEADS_PER_STEP, s // TQ),
        in_specs=[
            pl.BlockSpec((1, TQ, hw), lambda bi, hi, qi: (bi, qi, hi)),
            pl.BlockSpec((1, s, hw), lambda bi, hi, qi: (bi, 0, hi)),
            pl.BlockSpec((1, hw, s), lambda bi, hi, qi: (bi, hi, 0)),
            _const_spec((2, TK, 2 * TQ)),
            _const_spec((4, DA_HEAD_DIM)),
            _const_spec((DA_V_DIM, 1)),
        ],
        out_specs=pl.BlockSpec((1, TQ, hw), lambda bi, hi, qi: (bi, qi, hi)),
        out_shape=jax.ShapeDtypeStruct((b, s, V_WIDTH), BF16),
        scratch_shapes=[pltpu.VMEM((HEADS_PER_STEP, DA_V_DIM + BF16_SUBLANES, 2 * TQ), F32),
                        pltpu.VMEM((TK, 2 * TQ), F32),
                        pltpu.VMEM((TK, 2 * TQ), F32)],
        name="diff_attn",
        compiler_params=pltpu.CompilerParams(
            dimension_semantics=("arbitrary", "arbitrary", "arbitrary"),
            vmem_limit_bytes=V7X_VMEM_LIMIT_BYTES),
    )(q, k, vt, bias, lamv, subln_g_col)


def _merge_kernel(h_ref, o_ref, gy_ref, ga_ref, wao_ref, wo_ref, out_ref):
    y_attn = _mm(o_ref[...], wao_ref[...])
    mix = gy_ref[...].astype(F32) + ga_ref[...].astype(F32) * y_attn
    out_ref[...] = h_ref[...] + _mm(mix.astype(BF16), wo_ref[...])


def _merge(h2d, o2d, gy2d, ga2d, wao, wo):
    t = h2d.shape[0]
    tm = TM_MERGE
    tok = lambda i: (i, 0)
    return pl.pallas_call(
        _merge_kernel,
        grid=(t // tm,),
        in_specs=[
            pl.BlockSpec((tm, D_MODEL), tok),
            pl.BlockSpec((tm, V_WIDTH), tok),
            pl.BlockSpec((tm, D_MODEL), tok),
            pl.BlockSpec((tm, D_MODEL), tok),
            _const_spec((V_WIDTH, D_MODEL)),
            _const_spec((D_MODEL, D_MODEL)),
        ],
        out_specs=pl.BlockSpec((tm, D_MODEL), tok),
        out_shape=jax.ShapeDtypeStruct((t, D_MODEL), F32),
        name="merge",
        compiler_params=pltpu.CompilerParams(
            dimension_semantics=("arbitrary",), vmem_limit_bytes=V7X_VMEM_LIMIT_BYTES),
    )(h2d, o2d, gy2d, ga2d, wao, wo)


def _rope_tables(seq):
    half = DA_HEAD_DIM // 2
    inv_freq = 1.0 / (ROPE_THETA ** (jnp.arange(0, DA_HEAD_DIM, 2, dtype=F32) / DA_HEAD_DIM))
    ang = jnp.arange(seq, dtype=F32)[:, None] * inv_freq[None, :]
    cos, sin = jnp.cos(ang), jnp.sin(ang)
    cos128 = jnp.tile(cos, (1, 128 // half))
    sin128 = jnp.tile(jnp.concatenate([-sin, sin], axis=1), (1, 128 // DA_HEAD_DIM))
    return cos128, sin128


def kernel(x, norm_ffn1, ffn1_gate, ffn1_up, ffn1_down, norm_mix, w_in, b_gate, conv_w,
           w_conv_out, lambda_q1, lambda_k1, lambda_q2, lambda_k2, subln_g, w_attn_out, w_o,
           norm_ffn2, ffn2_gate, ffn2_up, ffn2_down, norm_final):
    bsz, seq, d = x.shape
    assert d == D_MODEL and norm_ffn1.shape[0] == 1
    assert seq % TQ == 0 and seq % TM_PROJ == 0 and (bsz * seq) % TM_FFN == 0
    assert TQ == TK and TQ % CHUNK == 0
    assert HEADS_PER_STEP == 2 and DA_HEADS % HEADS_PER_STEP == 0
    t = bsz * seq
    l = 0
    bf = lambda a: a.astype(BF16)
    row = lambda a: a.reshape(1, -1)

    cos128, sin128 = _rope_tables(seq)
    scale = math.log2(math.e) / math.sqrt(DA_HEAD_DIM)
    pos_chunk = jnp.arange(TQ, dtype=jnp.int32) // CHUNK
    bias = jnp.where(pos_chunk[:, None] <= pos_chunk[None, :], 0.0, -jnp.inf).astype(F32)
    bias = jnp.tile(bias, (1, 2))
    bias = jnp.stack([jnp.zeros_like(bias), bias])
    lamv = jnp.stack([lambda_q1[l], lambda_k1[l], lambda_q2[l], lambda_k2[l]]).astype(F32)

    h = _ffn(x.reshape(t, d), row(norm_ffn1[l]), bf(ffn1_gate[l]), bf(ffn1_up[l]),
             bf(ffn1_down[l]), row(norm_final), final_norm=False)
    gy, ga, q, k, vt = _in_proj(
        h.reshape(bsz, seq, d), row(norm_mix[l]), bf(w_in[l]), row(b_gate[l]), conv_w[l],
        bf(w_conv_out[l]), cos128 * scale, sin128 * scale, cos128, sin128)
    o = _diff_attn(q, k, vt, bias, lamv, subln_g[l].reshape(-1, 1))
    h2 = _merge(h, o.reshape(t, V_WIDTH), gy.reshape(t, d), ga.reshape(t, d),
                bf(w_attn_out[l]), bf(w_o[l]))
    out = _ffn(h2, row(norm_ffn2[l]), bf(ffn2_gate[l]), bf(ffn2_up[l]), bf(ffn2_down[l]),
               row(norm_final), final_norm=True)
    return out.reshape(bsz, seq, d)
```

```python
import math

import jax
import jax.numpy as jnp
from jax import lax
from jax.experimental import pallas as pl
from jax.experimental.pallas import tpu as pltpu

D_MODEL = 1024
CHUNK = 64
CONV_WIDTH = 512
CONV_K = 3
DA_HEADS = 4
DA_HEAD_DIM = 64
DA_V_DIM = 128
QK_WIDTH = 512
V_WIDTH = 512
D_FF = 2816
ROPE_THETA = 10000.0
NORM_EPS = 1e-6
LAMBDA_INIT = 0.8 - 0.6 * math.exp(-0.3 * 0)

OFF_XC, OFF_BG, OFF_CG, OFF_Q, OFF_K, OFF_V, OFF_GC, OFF_GA = (
    0, 512, 1024, 1536, 2048, 2560, 3072, 4096)
IN_WIDTH = 5120

V7X_VMEM_LIMIT_BYTES = 56 * 1024 * 1024
BF16_SUBLANES = 16

TM_FFN = 512
TM_PROJ = 512
TQ = 512
TK = 512
HEADS_PER_STEP = 2
CONV_HALO = 8

BF16 = jnp.bfloat16
F32 = jnp.float32


def _const_spec(shape):
    return pl.BlockSpec(shape, lambda *_: (0,) * len(shape), pipeline_mode=pl.Buffered(1))


def _rms(x, g):
    return x * lax.rsqrt(jnp.mean(x * x, axis=-1, keepdims=True) + NORM_EPS) * g


def _sigmoid(x):
    return 1.0 / (1.0 + jnp.exp(-x))


def _mm(a, b):
    return jnp.dot(a, b, preferred_element_type=F32)


def _mm_nt(a, b):
    return lax.dot_general(a, b, (((1,), (1,)), ((), ())), preferred_element_type=F32)


def _half_swiglu_step(x, g_ref, wg_ref, wu_ref, wd_ref):
    xn = _rms(x, g_ref[...]).astype(BF16)
    gate = _mm(xn, wg_ref[...])
    up = _mm(xn, wu_ref[...])
    a = (gate * _sigmoid(gate) * up).astype(BF16)
    return x + 0.5 * _mm(a, wd_ref[...])


def _ffn_kernel(x_ref, g_ref, wg_ref, wu_ref, wd_ref, o_ref):
    o_ref[...] = _half_swiglu_step(x_ref[...], g_ref, wg_ref, wu_ref, wd_ref)


def _merge_ffn_kernel(h_ref, o_ref, gy_ref, ga_ref, wao_ref, wo_ref, g_ref, wg_ref, wu_ref,
                      wd_ref, gf_ref, out_ref):
    y_attn = _mm(o_ref[...], wao_ref[...])
    mix = gy_ref[...].astype(F32) + ga_ref[...].astype(F32) * y_attn
    h2 = h_ref[...] + _mm(mix.astype(BF16), wo_ref[...])
    out_ref[...] = _rms(_half_swiglu_step(h2, g_ref, wg_ref, wu_ref, wd_ref), gf_ref[...])


_FFN_WEIGHT_SPECS = lambda: [
    _const_spec((1, D_MODEL)),
    _const_spec((D_MODEL, D_FF)),
    _const_spec((D_MODEL, D_FF)),
    _const_spec((D_FF, D_MODEL)),
]


def _ffn(x2d, g, wg, wu, wd):
    t = x2d.shape[0]
    tok = lambda i: (i, 0)
    return pl.pallas_call(
        _ffn_kernel,
        grid=(t // TM_FFN,),
        in_specs=[pl.BlockSpec((TM_FFN, D_MODEL), tok)] + _FFN_WEIGHT_SPECS(),
        out_specs=pl.BlockSpec((TM_FFN, D_MODEL), tok),
        out_shape=jax.ShapeDtypeStruct((t, D_MODEL), F32),
        name="ffn",
        compiler_params=pltpu.CompilerParams(
            dimension_semantics=("arbitrary",), vmem_limit_bytes=V7X_VMEM_LIMIT_BYTES),
    )(x2d, g, wg, wu, wd)


def _merge_ffn(h2d, o2d, gy2d, ga2d, wao, wo, g, wg, wu, wd, gf):
    t = h2d.shape[0]
    tm = TM_FFN
    tok = lambda i: (i, 0)
    return pl.pallas_call(
        _merge_ffn_kernel,
        grid=(t // tm,),
        in_specs=[
            pl.BlockSpec((tm, D_MODEL), tok),
            pl.BlockSpec((tm, V_WIDTH), tok),
            pl.BlockSpec((tm, D_MODEL), tok),
            pl.BlockSpec((tm, D_MODEL), tok),
            _const_spec((V_WIDTH, D_MODEL)),
            _const_spec((D_MODEL, D_MODEL)),
        ] + _FFN_WEIGHT_SPECS() + [_const_spec((1, D_MODEL))],
        out_specs=pl.BlockSpec((tm, D_MODEL), tok),
        out_shape=jax.ShapeDtypeStruct((t, D_MODEL), F32),
        name="merge_ffn_final",
        compiler_params=pltpu.CompilerParams(
            dimension_semantics=("arbitrary",), vmem_limit_bytes=V7X_VMEM_LIMIT_BYTES),
    )(h2d, o2d, gy2d, ga2d, wao, wo, g, wg, wu, wd, gf)


def _rope(x, cos, sin_signed, lo_half):
    outs = []
    for c in range(x.shape[1] // 128):
        xc = x[:, c * 128:(c + 1) * 128]
        rot = jnp.where(lo_half, pltpu.roll(xc, 96, 1), pltpu.roll(xc, 32, 1))
        outs.append(xc * cos + rot * sin_signed)
    return jnp.concatenate(outs, axis=1)


def _in_proj_kernel(h_ref, g_ref, w_ref, bgate_ref, convw_ref, wco_ref, cosq_ref, sinq_ref,
                    cosk_ref, sink_ref, gy_ref, ga_ref, q_ref, k_ref, vt_ref, zbuf_ref):
    si = pl.program_id(1)
    tm = h_ref.shape[1]
    u = _rms(h_ref[0], g_ref[...]).astype(BF16)

    def proj(off, width):
        return _mm(u, w_ref[:, off:off + width])

    z = proj(OFF_CG, CONV_WIDTH) * proj(OFF_XC, CONV_WIDTH)

    @pl.when(si == 0)
    def _():
        zbuf_ref[0:CONV_HALO, :] = jnp.zeros((CONV_HALO, CONV_WIDTH), F32)

    zbuf_ref[CONV_HALO:CONV_HALO + tm, :] = z
    cw = convw_ref[...]
    zc = (cw[0:1, :] * zbuf_ref[CONV_HALO - 2:CONV_HALO - 2 + tm, :]
          + cw[1:2, :] * zbuf_ref[CONV_HALO - 1:CONV_HALO - 1 + tm, :]
          + cw[2:3, :] * z)
    zbuf_ref[0:CONV_HALO, :] = z[tm - CONV_HALO:, :]
    ya = _mm((proj(OFF_BG, CONV_WIDTH) * zc).astype(BF16), wco_ref[...])
    g_conv = _sigmoid(proj(OFF_GC, D_MODEL) + bgate_ref[:, 0:D_MODEL])
    gy_ref[0] = (g_conv * ya).astype(gy_ref.dtype)
    ga_ref[0] = _sigmoid(proj(OFF_GA, D_MODEL) + bgate_ref[:, D_MODEL:]).astype(ga_ref.dtype)

    lane = lax.broadcasted_iota(jnp.int32, (tm, 128), 1)
    lo_half = (lane % DA_HEAD_DIM) < (DA_HEAD_DIM // 2)
    q_ref[0] = _rope(proj(OFF_Q, QK_WIDTH), cosq_ref[...], sinq_ref[...], lo_half).astype(BF16)
    k_ref[0] = _rope(proj(OFF_K, QK_WIDTH), cosk_ref[...], sink_ref[...], lo_half).astype(BF16)
    vt_ref[0] = proj(OFF_V, V_WIDTH).T.astype(BF16)


def _in_proj(h3d, g, w_in, b_gate, conv_w, w_conv_out, cosq, sinq, cosk, sink):
    b, s, _ = h3d.shape
    tm = TM_PROJ
    tok = lambda bi, si: (bi, si, 0)
    tab = lambda bi, si: (si, 0)
    bsd = lambda width, dt: jax.ShapeDtypeStruct((b, s, width), dt)
    return pl.pallas_call(
        _in_proj_kernel,
        grid=(b, s // tm),
        in_specs=[
            pl.BlockSpec((1, tm, D_MODEL), tok),
            _const_spec((1, D_MODEL)),
            _const_spec((D_MODEL, IN_WIDTH)),
            _const_spec((1, 2 * D_MODEL)),
            _const_spec((CONV_K, CONV_WIDTH)),
            _const_spec((CONV_WIDTH, D_MODEL)),
            pl.BlockSpec((tm, 128), tab),
            pl.BlockSpec((tm, 128), tab),
            pl.BlockSpec((tm, 128), tab),
            pl.BlockSpec((tm, 128), tab),
        ],
        out_specs=[
            pl.BlockSpec((1, tm, D_MODEL), tok),
            pl.BlockSpec((1, tm, D_MODEL), tok),
            pl.BlockSpec((1, tm, QK_WIDTH), tok),
            pl.BlockSpec((1, tm, QK_WIDTH), tok),
            pl.BlockSpec((1, V_WIDTH, tm), lambda bi, si: (bi, 0, si)),
        ],
        out_shape=[bsd(D_MODEL, BF16), bsd(D_MODEL, BF16), bsd(QK_WIDTH, BF16),
                   bsd(QK_WIDTH, BF16), jax.ShapeDtypeStruct((b, V_WIDTH, s), BF16)],
        scratch_shapes=[pltpu.VMEM((CONV_HALO + tm, CONV_WIDTH), F32)],
        name="in_proj",
        compiler_params=pltpu.CompilerParams(
            dimension_semantics=("arbitrary", "arbitrary"),
            vmem_limit_bytes=V7X_VMEM_LIMIT_BYTES),
    )(h3d, g, w_in, b_gate, conv_w, w_conv_out, cosq, sinq, cosk, sink)


def _attn_kernel(q_ref, k_ref, vt_ref, bias_ref, lamv_ref, sg_ref, o_ref,
                 acc_ref, sa_ref, sb_ref):
    qi = pl.program_id(2)
    tq = q_ref.shape[1]
    heads = range(HEADS_PER_STEP)
    hcols = lambda hh: slice(hh * 128, (hh + 1) * 128)

    lane = lax.broadcasted_iota(jnp.int32, (tq, 128), 1)
    zero = jnp.zeros((tq, 128), BF16)
    qq = []
    for hh in heads:
        q = q_ref[0, :, hcols(hh)]
        qq.append(jnp.concatenate([jnp.where(lane < DA_HEAD_DIM, q, zero),
                                   jnp.where(lane >= DA_HEAD_DIM, q, zero)], axis=0))
    ones = jnp.ones((BF16_SUBLANES, TK), BF16)

    def scores(hh, tile, s_ref, masked):
        kt = k_ref[0, pl.ds(pl.multiple_of(tile * TK, TK), TK), hcols(hh)]
        s = _mm_nt(kt, qq[hh])
        if masked:
            s = s + bias_ref[...]
        s_ref[...] = s
        return jnp.max(s, axis=0, keepdims=True)

    def accumulate(hh, tile, s_ref, s_max, m):
        start = pl.multiple_of(tile * TK, TK)
        vt = jnp.concatenate([vt_ref[0, hcols(hh), pl.ds(start, TK)], ones], axis=0)
        m_new = jnp.maximum(m, s_max)
        p = jnp.exp2((s_ref[...] - m_new).astype(BF16))
        acc_ref[hh] = jnp.exp2(m - m_new) * acc_ref[hh] + _mm(vt, p)
        return m_new

    acc_ref[...] = jnp.zeros_like(acc_ref)
    sa_max = scores(0, qi, sa_ref, True)

    def body(j, carry):
        ma, mb, sa_max = carry
        sb_max = scores(1, j, sb_ref, False)
        ma = accumulate(0, jnp.where(j == 0, qi, j - 1), sa_ref, sa_max, ma)
        sa_max = scores(0, j, sa_ref, False)
        mb = accumulate(1, j, sb_ref, sb_max, mb)
        return ma, mb, sa_max

    m0 = jnp.full((1, 2 * tq), -jnp.inf, F32)
    carry = lax.fori_loop(0, qi // 2, lambda i, c: body(2 * i + 1, body(2 * i, c)),
                          (m0, m0, sa_max))
    ma, mb, sa_max = lax.cond(qi % 2 == 1, lambda c: body(qi - 1, c), lambda c: c, carry)
    sb_max = scores(1, qi, sb_ref, True)
    accumulate(0, jnp.maximum(qi - 1, 0), sa_ref, sa_max, ma)
    accumulate(1, qi, sb_ref, sb_max, mb)

    lv = lamv_ref[...]
    lam = (jnp.exp(jnp.sum(lv[0:1] * lv[1:2], axis=-1, keepdims=True))
           - jnp.exp(jnp.sum(lv[2:3] * lv[3:4], axis=-1, keepdims=True)) + LAMBDA_INIT)
    for hh in heads:
        o = acc_ref[hh, 0:DA_V_DIM, :] / acc_ref[hh, DA_V_DIM:DA_V_DIM + 1, :]
        o = o[:, :tq] - lam * o[:, tq:]
        o = o * lax.rsqrt(jnp.mean(o * o, axis=0, keepdims=True) + NORM_EPS)
        o = o * (sg_ref[...] * (1.0 - LAMBDA_INIT))
        o_ref[0, :, hcols(hh)] = o.T.astype(o_ref.dtype)


def _diff_attn(q, k, vt, bias, lamv, subln_g_col):
    b, s, _ = q.shape
    hw = 128 * HEADS_PER_STEP
    return pl.pallas_call(
        _attn_kernel,
        grid=(b, DA_HEADS // HEADS_PER_STEP, s // TQ),
        in_specs=[
            pl.BlockSpec((1, TQ, hw), lambda bi, hi, qi: (bi, qi, hi)),
            pl.BlockSpec((1, s, hw), lambda bi, hi, qi: (bi, 0, hi)),
            pl.BlockSpec((1, hw, s), lambda bi, hi, qi: (bi, hi, 0)),
            _const_spec((TK, 2 * TQ)),
            _const_spec((4, DA_HEAD_DIM)),
            _const_spec((DA_V_DIM, 1)),
        ],
        out_specs=pl.BlockSpec((1, TQ, hw), lambda bi, hi, qi: (bi, qi, hi)),
        out_shape=jax.ShapeDtypeStruct((b, s, V_WIDTH), BF16),
        scratch_shapes=[pltpu.VMEM((HEADS_PER_STEP, DA_V_DIM + BF16_SUBLANES, 2 * TQ), F32),
                        pltpu.VMEM((TK, 2 * TQ), F32),
                        pltpu.VMEM((TK, 2 * TQ), F32)],
        name="diff_attn",
        compiler_params=pltpu.CompilerParams(
            dimension_semantics=("arbitrary", "arbitrary", "arbitrary"),
            vmem_limit_bytes=V7X_VMEM_LIMIT_BYTES),
    )(q, k, vt, bias, lamv, subln_g_col)


def _rope_tables(seq):
    half = DA_HEAD_DIM // 2
    inv_freq = 1.0 / (ROPE_THETA ** (jnp.arange(0, DA_HEAD_DIM, 2, dtype=F32) / DA_HEAD_DIM))
    ang = jnp.arange(seq, dtype=F32)[:, None] * inv_freq[None, :]
    cos, sin = jnp.cos(ang), jnp.sin(ang)
    cos128 = jnp.tile(cos, (1, 128 // half))
    sin128 = jnp.tile(jnp.concatenate([-sin, sin], axis=1), (1, 128 // DA_HEAD_DIM))
    return cos128, sin128


def kernel(x, norm_ffn1, ffn1_gate, ffn1_up, ffn1_down, norm_mix, w_in, b_gate, conv_w,
           w_conv_out, lambda_q1, lambda_k1, lambda_q2, lambda_k2, subln_g, w_attn_out, w_o,
           norm_ffn2, ffn2_gate, ffn2_up, ffn2_down, norm_final):
    bsz, seq, d = x.shape
    assert d == D_MODEL and norm_ffn1.shape[0] == 1
    assert seq % TQ == 0 and seq % TM_PROJ == 0 and (bsz * seq) % TM_FFN == 0
    assert TQ == TK and TQ % CHUNK == 0
    assert HEADS_PER_STEP == 2 and DA_HEADS % HEADS_PER_STEP == 0
    t = bsz * seq
    l = 0
    bf = lambda a: a.astype(BF16)
    row = lambda a: a.reshape(1, -1)

    cos128, sin128 = _rope_tables(seq)
    scale = math.log2(math.e) / math.sqrt(DA_HEAD_DIM)
    pos_chunk = jnp.arange(TQ, dtype=jnp.int32) // CHUNK
    bias = jnp.where(pos_chunk[:, None] <= pos_chunk[None, :], 0.0, -jnp.inf).astype(F32)
    bias = jnp.tile(bias, (1, 2))
    lamv = jnp.stack([lambda_q1[l], lambda_k1[l], lambda_q2[l], lambda_k2[l]]).astype(F32)

    h = _ffn(x.reshape(t, d), row(norm_ffn1[l]), bf(ffn1_gate[l]), bf(ffn1_up[l]),
             bf(ffn1_down[l]))
    gy, ga, q, k, vt = _in_proj(
        h.reshape(bsz, seq, d), row(norm_mix[l]), bf(w_in[l]), row(b_gate[l]), conv_w[l],
        bf(w_conv_out[l]), cos128 * scale, sin128 * scale, cos128, sin128)
    o = _diff_attn(q, k, vt, bias, lamv, subln_g[l].reshape(-1, 1))
    out = _merge_ffn(h, o.reshape(t, V_WIDTH), gy.reshape(t, d), ga.reshape(t, d),
                     bf(w_attn_out[l]), bf(w_o[l]), row(norm_ffn2[l]), bf(ffn2_gate[l]),
                     bf(ffn2_up[l]), bf(ffn2_down[l]), row(norm_final))
    return out.reshape(bsz, seq, d)
```

```python
import math

import jax
import jax.numpy as jnp
from jax import lax
from jax.experimental import pallas as pl
from jax.experimental.pallas import tpu as pltpu

D_MODEL = 1024
CHUNK = 64
CONV_WIDTH = 512
CONV_K = 3
DA_HEADS = 4
DA_HEAD_DIM = 64
DA_V_DIM = 128
QK_WIDTH = 512
V_WIDTH = 512
D_FF = 2816
ROPE_THETA = 10000.0
NORM_EPS = 1e-6
LAMBDA_INIT = 0.8 - 0.6 * math.exp(-0.3 * 0)

OFF_XC, OFF_BG, OFF_CG, OFF_Q, OFF_K, OFF_V, OFF_GC, OFF_GA = (
    0, 512, 1024, 1536, 2048, 2560, 3072, 4096)
IN_WIDTH = 5120

V7X_VMEM_LIMIT_BYTES = 56 * 1024 * 1024
BF16_SUBLANES = 16

TM_FFN = 512
TM_PROJ = 512
TQ = 512
TK = 512
HEADS_PER_STEP = 2
CONV_HALO = 8

BF16 = jnp.bfloat16
F32 = jnp.float32


def _const_spec(shape):
    return pl.BlockSpec(shape, lambda *_: (0,) * len(shape), pipeline_mode=pl.Buffered(1))


def _rms(x, g):
    return x * lax.rsqrt(jnp.mean(x * x, axis=-1, keepdims=True) + NORM_EPS) * g


def _sigmoid(x):
    return 0.5 * jnp.tanh(0.5 * x) + 0.5


def _mm(a, b):
    return jnp.dot(a, b, preferred_element_type=F32)


def _mm_nt(a, b):
    return lax.dot_general(a, b, (((1,), (1,)), ((), ())), preferred_element_type=F32)


def _half_swiglu_step(x, g_ref, wg_ref, wu_ref, wd_ref):
    xn = _rms(x, g_ref[...]).astype(BF16)
    gate = _mm(xn, wg_ref[...])
    up = _mm(xn, wu_ref[...])
    a = (gate * _sigmoid(gate) * up).astype(BF16)
    return x + 0.5 * _mm(a, wd_ref[...])


def _ffn_kernel(x_ref, g_ref, wg_ref, wu_ref, wd_ref, o_ref):
    o_ref[...] = _half_swiglu_step(x_ref[...], g_ref, wg_ref, wu_ref, wd_ref)


def _merge_ffn_kernel(h_ref, o_ref, gy_ref, ga_ref, wao_ref, wo_ref, g_ref, wg_ref, wu_ref,
                      wd_ref, gf_ref, out_ref):
    y_attn = _mm(o_ref[...], wao_ref[...])
    mix = gy_ref[...].astype(F32) + ga_ref[...].astype(F32) * y_attn
    h2 = h_ref[...] + _mm(mix.astype(BF16), wo_ref[...])
    out_ref[...] = _rms(_half_swiglu_step(h2, g_ref, wg_ref, wu_ref, wd_ref), gf_ref[...])


_FFN_WEIGHT_SPECS = lambda: [
    _const_spec((1, D_MODEL)),
    _const_spec((D_MODEL, D_FF)),
    _const_spec((D_MODEL, D_FF)),
    _const_spec((D_FF, D_MODEL)),
]


def _ffn(x2d, g, wg, wu, wd):
    t = x2d.shape[0]
    tok = lambda i: (i, 0)
    return pl.pallas_call(
        _ffn_kernel,
        grid=(t // TM_FFN,),
        in_specs=[pl.BlockSpec((TM_FFN, D_MODEL), tok)] + _FFN_WEIGHT_SPECS(),
        out_specs=pl.BlockSpec((TM_FFN, D_MODEL), tok),
        out_shape=jax.ShapeDtypeStruct((t, D_MODEL), F32),
        name="ffn",
        compiler_params=pltpu.CompilerParams(
            dimension_semantics=("arbitrary",), vmem_limit_bytes=V7X_VMEM_LIMIT_BYTES),
    )(x2d, g, wg, wu, wd)


def _merge_ffn(h2d, o2d, gy2d, ga2d, wao, wo, g, wg, wu, wd, gf):
    t = h2d.shape[0]
    tm = TM_FFN
    tok = lambda i: (i, 0)
    return pl.pallas_call(
        _merge_ffn_kernel,
        grid=(t // tm,),
        in_specs=[
            pl.BlockSpec((tm, D_MODEL), tok),
            pl.BlockSpec((tm, V_WIDTH), tok),
            pl.BlockSpec((tm, D_MODEL), tok),
            pl.BlockSpec((tm, D_MODEL), tok),
            _const_spec((V_WIDTH, D_MODEL)),
            _const_spec((D_MODEL, D_MODEL)),
        ] + _FFN_WEIGHT_SPECS() + [_const_spec((1, D_MODEL))],
        out_specs=pl.BlockSpec((tm, D_MODEL), tok),
        out_shape=jax.ShapeDtypeStruct((t, D_MODEL), F32),
        name="merge_ffn_final",
        compiler_params=pltpu.CompilerParams(
            dimension_semantics=("arbitrary",), vmem_limit_bytes=V7X_VMEM_LIMIT_BYTES),
    )(h2d, o2d, gy2d, ga2d, wao, wo, g, wg, wu, wd, gf)


def _rope(x, cos, sin_signed, lo_half):
    outs = []
    for c in range(x.shape[1] // 128):
        xc = x[:, c * 128:(c + 1) * 128]
        rot = jnp.where(lo_half, pltpu.roll(xc, 96, 1), pltpu.roll(xc, 32, 1))
        outs.append(xc * cos + rot * sin_signed)
    return jnp.concatenate(outs, axis=1)


def _in_proj_kernel(h_ref, g_ref, w_ref, bgate_ref, convw_ref, wco_ref, cosq_ref, sinq_ref,
                    cosk_ref, sink_ref, gy_ref, ga_ref, q_ref, k_ref, vt_ref, zbuf_ref):
    tm = h_ref.shape[1]

    @pl.when(pl.program_id(1) == 0)
    def _():
        zbuf_ref[0:CONV_HALO, :] = jnp.zeros((CONV_HALO, CONV_WIDTH), F32)

    u = _rms(h_ref[0], g_ref[...]).astype(BF16)

    def proj(off, width):
        return _mm(u, w_ref[:, off:off + width])

    vt_ref[0] = proj(OFF_V, V_WIDTH).T.astype(BF16)
    lane = lax.broadcasted_iota(jnp.int32, (tm, 128), 1)
    lo_half = (lane % DA_HEAD_DIM) < (DA_HEAD_DIM // 2)
    q_ref[0] = _rope(proj(OFF_Q, QK_WIDTH), cosq_ref[...], sinq_ref[...], lo_half).astype(BF16)
    k_ref[0] = _rope(proj(OFF_K, QK_WIDTH), cosk_ref[...], sink_ref[...], lo_half).astype(BF16)

    z = proj(OFF_CG, CONV_WIDTH) * proj(OFF_XC, CONV_WIDTH)
    zbuf_ref[CONV_HALO:CONV_HALO + tm, :] = z
    cw = convw_ref[...]
    zc = (cw[0:1, :] * zbuf_ref[CONV_HALO - 2:CONV_HALO - 2 + tm, :]
          + cw[1:2, :] * zbuf_ref[CONV_HALO - 1:CONV_HALO - 1 + tm, :]
          + cw[2:3, :] * z)
    zbuf_ref[0:CONV_HALO, :] = z[tm - CONV_HALO:, :]
    ya = _mm((proj(OFF_BG, CONV_WIDTH) * zc).astype(BF16), wco_ref[...])
    g_conv = _sigmoid(proj(OFF_GC, D_MODEL) + bgate_ref[:, 0:D_MODEL])
    gy_ref[0] = (g_conv * ya).astype(gy_ref.dtype)
    ga_ref[0] = _sigmoid(proj(OFF_GA, D_MODEL) + bgate_ref[:, D_MODEL:]).astype(ga_ref.dtype)


def _in_proj(h3d, g, w_in, b_gate, conv_w, w_conv_out, cosq, sinq, cosk, sink):
    b, s, _ = h3d.shape
    tm = TM_PROJ
    tok = lambda bi, si: (bi, si, 0)
    tab = lambda bi, si: (si, 0)
    bsd = lambda width, dt: jax.ShapeDtypeStruct((b, s, width), dt)
    return pl.pallas_call(
        _in_proj_kernel,
        grid=(b, s // tm),
        in_specs=[
            pl.BlockSpec((1, tm, D_MODEL), tok),
            _const_spec((1, D_MODEL)),
            _const_spec((D_MODEL, IN_WIDTH)),
            _const_spec((1, 2 * D_MODEL)),
            _const_spec((CONV_K, CONV_WIDTH)),
            _const_spec((CONV_WIDTH, D_MODEL)),
            pl.BlockSpec((tm, 128), tab),
            pl.BlockSpec((tm, 128), tab),
            pl.BlockSpec((tm, 128), tab),
            pl.BlockSpec((tm, 128), tab),
        ],
        out_specs=[
            pl.BlockSpec((1, tm, D_MODEL), tok),
            pl.BlockSpec((1, tm, D_MODEL), tok),
            pl.BlockSpec((1, tm, QK_WIDTH), tok),
            pl.BlockSpec((1, tm, QK_WIDTH), tok),
            pl.BlockSpec((1, V_WIDTH, tm), lambda bi, si: (bi, 0, si)),
        ],
        out_shape=[bsd(D_MODEL, BF16), bsd(D_MODEL, BF16), bsd(QK_WIDTH, BF16),
                   bsd(QK_WIDTH, BF16), jax.ShapeDtypeStruct((b, V_WIDTH, s), BF16)],
        scratch_shapes=[pltpu.VMEM((CONV_HALO + tm, CONV_WIDTH), F32)],
        name="in_proj",
        compiler_params=pltpu.CompilerParams(
            dimension_semantics=("arbitrary", "arbitrary"),
            vmem_limit_bytes=V7X_VMEM_LIMIT_BYTES),
    )(h3d, g, w_in, b_gate, conv_w, w_conv_out, cosq, sinq, cosk, sink)


def _attn_kernel(q_ref, k_ref, vt_ref, bias_ref, lamv_ref, sg_ref, o_ref,
                 acc_ref, sa_ref, sb_ref):
    qi = pl.program_id(2)
    tq = q_ref.shape[1]
    heads = range(HEADS_PER_STEP)
    hcols = lambda hh: slice(hh * 128, (hh + 1) * 128)

    feat = lax.broadcasted_iota(jnp.int32, (128, tq), 0)
    zero = jnp.zeros((128, tq), F32)
    qqt = []
    for hh in heads:
        qt = q_ref[0, :, hcols(hh)].astype(F32).T
        qqt.append(jnp.concatenate([jnp.where(feat < DA_HEAD_DIM, qt, zero),
                                    jnp.where(feat >= DA_HEAD_DIM, qt, zero)],
                                   axis=1).astype(BF16))
    ones = jnp.ones((BF16_SUBLANES, TK), BF16)

    def scores(hh, tile, s_ref, masked):
        kt = k_ref[0, pl.ds(pl.multiple_of(tile * TK, TK), TK), hcols(hh)]
        s = _mm(kt, qqt[hh])
        if masked:
            s = s + bias_ref[...]
        s_ref[...] = s
        return jnp.max(s, axis=0, keepdims=True)

    def accumulate(hh, tile, s_ref, s_max, m):
        start = pl.multiple_of(tile * TK, TK)
        vt = jnp.concatenate([vt_ref[0, hcols(hh), pl.ds(start, TK)], ones], axis=0)
        m_new = jnp.maximum(m, s_max)
        p = jnp.exp2((s_ref[...] - m_new).astype(BF16))
        acc_ref[hh] = jnp.exp2(m - m_new) * acc_ref[hh] + _mm(vt, p)
        return m_new

    acc_ref[...] = jnp.zeros_like(acc_ref)
    sa_max = scores(0, qi, sa_ref, True)

    def body(j, carry):
        ma, mb, sa_max = carry
        sb_max = scores(1, j, sb_ref, False)
        ma = accumulate(0, jnp.where(j == 0, qi, j - 1), sa_ref, sa_max, ma)
        sa_max = scores(0, j, sa_ref, False)
        mb = accumulate(1, j, sb_ref, sb_max, mb)
        return ma, mb, sa_max

    m0 = jnp.full((1, 2 * tq), -jnp.inf, F32)
    carry = lax.fori_loop(0, qi // 2, lambda i, c: body(2 * i + 1, body(2 * i, c)),
                          (m0, m0, sa_max))
    ma, mb, sa_max = lax.cond(qi % 2 == 1, lambda c: body(qi - 1, c), lambda c: c, carry)
    sb_max = scores(1, qi, sb_ref, True)
    accumulate(0, jnp.maximum(qi - 1, 0), sa_ref, sa_max, ma)
    accumulate(1, qi, sb_ref, sb_max, mb)

    lv = lamv_ref[...]
    lam = (jnp.exp(jnp.sum(lv[0:1] * lv[1:2], axis=-1, keepdims=True))
           - jnp.exp(jnp.sum(lv[2:3] * lv[3:4], axis=-1, keepdims=True)) + LAMBDA_INIT)
    for hh in heads:
        o = acc_ref[hh, 0:DA_V_DIM, :] / acc_ref[hh, DA_V_DIM:DA_V_DIM + 1, :]
        o = o[:, :tq] - lam * o[:, tq:]
        o = o * lax.rsqrt(jnp.mean(o * o, axis=0, keepdims=True) + NORM_EPS)
        o = o * (sg_ref[...] * (1.0 - LAMBDA_INIT))
        o_ref[0, :, hcols(hh)] = o.T.astype(o_ref.dtype)


def _diff_attn(q, k, vt, bias, lamv, subln_g_col):
    b, s, _ = q.shape
    hw = 128 * HEADS_PER_STEP
    return pl.pallas_call(
        _attn_kernel,
        grid=(b, DA_HEADS // HEADS_PER_STEP, s // TQ),
        in_specs=[
            pl.BlockSpec((1, TQ, hw), lambda bi, hi, qi: (bi, qi, hi)),
            pl.BlockSpec((1, s, hw), lambda bi, hi, qi: (bi, 0, hi)),
            pl.BlockSpec((1, hw, s), lambda bi, hi, qi: (bi, hi, 0)),
            _const_spec((TK, 2 * TQ)),
            _const_spec((4, DA_HEAD_DIM)),
            _const_spec((DA_V_DIM, 1)),
        ],
        out_specs=pl.BlockSpec((1, TQ, hw), lambda bi, hi, qi: (bi, qi, hi)),
        out_shape=jax.ShapeDtypeStruct((b, s, V_WIDTH), BF16),
        scratch_shapes=[pltpu.VMEM((HEADS_PER_STEP, DA_V_DIM + BF16_SUBLANES, 2 * TQ), F32),
                        pltpu.VMEM((TK, 2 * TQ), F32),
                        pltpu.VMEM((TK, 2 * TQ), F32)],
        name="diff_attn",
        compiler_params=pltpu.CompilerParams(
            dimension_semantics=("arbitrary", "arbitrary", "arbitrary"),
            vmem_limit_bytes=V7X_VMEM_LIMIT_BYTES),
    )(q, k, vt, bias, lamv, subln_g_col)


def _rope_tables(seq):
    half = DA_HEAD_DIM // 2
    inv_freq = 1.0 / (ROPE_THETA ** (jnp.arange(0, DA_HEAD_DIM, 2, dtype=F32) / DA_HEAD_DIM))
    ang = jnp.arange(seq, dtype=F32)[:, None] * inv_freq[None, :]
    cos, sin = jnp.cos(ang), jnp.sin(ang)
    cos128 = jnp.tile(cos, (1, 128 // half))
    sin128 = jnp.tile(jnp.concatenate([-sin, sin], axis=1), (1, 128 // DA_HEAD_DIM))
    return cos128, sin128


def kernel(x, norm_ffn1, ffn1_gate, ffn1_up, ffn1_down, norm_mix, w_in, b_gate, conv_w,
           w_conv_out, lambda_q1, lambda_k1, lambda_q2, lambda_k2, subln_g, w_attn_out, w_o,
           norm_ffn2, ffn2_gate, ffn2_up, ffn2_down, norm_final):
    bsz, seq, d = x.shape
    assert d == D_MODEL and norm_ffn1.shape[0] == 1
    assert seq % TQ == 0 and seq % TM_PROJ == 0 and (bsz * seq) % TM_FFN == 0
    assert TQ == TK and TQ % CHUNK == 0
    assert HEADS_PER_STEP == 2 and DA_HEADS % HEADS_PER_STEP == 0
    t = bsz * seq
    l = 0
    bf = lambda a: a.astype(BF16)
    row = lambda a: a.reshape(1, -1)

    cos128, sin128 = _rope_tables(seq)
    scale = math.log2(math.e) / math.sqrt(DA_HEAD_DIM)
    pos_chunk = jnp.arange(TQ, dtype=jnp.int32) // CHUNK
    bias = jnp.where(pos_chunk[:, None] <= pos_chunk[None, :], 0.0, -jnp.inf).astype(F32)
    bias = jnp.tile(bias, (1, 2))
    lamv = jnp.stack([lambda_q1[l], lambda_k1[l], lambda_q2[l], lambda_k2[l]]).astype(F32)

    h = _ffn(x.reshape(t, d), row(norm_ffn1[l]), bf(ffn1_gate[l]), bf(ffn1_up[l]),
             bf(ffn1_down[l]))
    gy, ga, q, k, vt = _in_proj(
        h.reshape(bsz, seq, d), row(norm_mix[l]), bf(w_in[l]), row(b_gate[l]), conv_w[l],
        bf(w_conv_out[l]), cos128 * scale, sin128 * scale, cos128, sin128)
    o = _diff_attn(q, k, vt, bias, lamv, subln_g[l].reshape(-1, 1))
    out = _merge_ffn(h, o.reshape(t, V_WIDTH), gy.reshape(t, d), ga.reshape(t, d),
                     bf(w_attn_out[l]), bf(w_o[l]), row(norm_ffn2[l]), bf(ffn2_gate[l]),
                     bf(ffn2_up[l]), bf(ffn2_down[l]), row(norm_final))
    return out.reshape(bsz, seq, d)
```

```python
import math

import jax
import jax.numpy as jnp
from jax import lax
from jax.experimental import pallas as pl
from jax.experimental.pallas import tpu as pltpu

D_MODEL = 1024
CHUNK = 64
CONV_WIDTH = 512
CONV_K = 3
DA_HEADS = 4
DA_HEAD_DIM = 64
DA_V_DIM = 128
QK_WIDTH = 512
V_WIDTH = 512
D_FF = 2816
ROPE_THETA = 10000.0
NORM_EPS = 1e-6
LAMBDA_INIT = 0.8 - 0.6 * math.exp(-0.3 * 0)

OFF_XC, OFF_BG, OFF_CG, OFF_Q, OFF_K, OFF_V, OFF_GC, OFF_GA = (
    0, 512, 1024, 1536, 2048, 2560, 3072, 4096)
IN_WIDTH = 5120

V7X_VMEM_LIMIT_BYTES = 56 * 1024 * 1024
BF16_SUBLANES = 16

TM_FFN = 1024
TM_MERGE_FFN = 512
TM_PROJ = 1024
TQ = 512
TK = 512
HEADS_PER_STEP = 2
CONV_HALO = 8

BF16 = jnp.bfloat16
F32 = jnp.float32


def _const_spec(shape):
    return pl.BlockSpec(shape, lambda *_: (0,) * len(shape), pipeline_mode=pl.Buffered(1))


def _rms(x, g):
    return x * lax.rsqrt(jnp.mean(x * x, axis=-1, keepdims=True) + NORM_EPS) * g


def _sigmoid(x):
    return 0.5 * jnp.tanh(0.5 * x) + 0.5


def _mm(a, b):
    return jnp.dot(a, b, preferred_element_type=F32)


def _mm_nt(a, b):
    return lax.dot_general(a, b, (((1,), (1,)), ((), ())), preferred_element_type=F32)


def _half_swiglu_step(x, g_ref, wg_ref, wu_ref, wd_ref):
    xn = _rms(x, g_ref[...]).astype(BF16)
    gate = _mm(xn, wg_ref[...])
    up = _mm(xn, wu_ref[...])
    a = (gate * _sigmoid(gate) * up).astype(BF16)
    return x + 0.5 * _mm(a, wd_ref[...])


def _ffn_kernel(x_ref, g_ref, wg_ref, wu_ref, wd_ref, o_ref):
    o_ref[...] = _half_swiglu_step(x_ref[...], g_ref, wg_ref, wu_ref, wd_ref)


def _merge_ffn_kernel(h_ref, o_ref, gy_ref, ga_ref, wao_ref, wo_ref, g_ref, wg_ref, wu_ref,
                      wd_ref, gf_ref, out_ref):
    y_attn = _mm(o_ref[...], wao_ref[...])
    mix = gy_ref[...].astype(F32) + ga_ref[...].astype(F32) * y_attn
    h2 = h_ref[...] + _mm(mix.astype(BF16), wo_ref[...])
    out_ref[...] = _rms(_half_swiglu_step(h2, g_ref, wg_ref, wu_ref, wd_ref), gf_ref[...])


_FFN_WEIGHT_SPECS = lambda: [
    _const_spec((1, D_MODEL)),
    _const_spec((D_MODEL, D_FF)),
    _const_spec((D_MODEL, D_FF)),
    _const_spec((D_FF, D_MODEL)),
]


def _ffn(x2d, g, wg, wu, wd):
    t = x2d.shape[0]
    tok = lambda i: (i, 0)
    return pl.pallas_call(
        _ffn_kernel,
        grid=(t // TM_FFN,),
        in_specs=[pl.BlockSpec((TM_FFN, D_MODEL), tok)] + _FFN_WEIGHT_SPECS(),
        out_specs=pl.BlockSpec((TM_FFN, D_MODEL), tok),
        out_shape=jax.ShapeDtypeStruct((t, D_MODEL), F32),
        name="ffn",
        compiler_params=pltpu.CompilerParams(
            dimension_semantics=("arbitrary",), vmem_limit_bytes=V7X_VMEM_LIMIT_BYTES),
    )(x2d, g, wg, wu, wd)


def _merge_ffn(h2d, o2d, gy2d, ga2d, wao, wo, g, wg, wu, wd, gf):
    t = h2d.shape[0]
    tm = TM_MERGE_FFN
    tok = lambda i: (i, 0)
    return pl.pallas_call(
        _merge_ffn_kernel,
        grid=(t // tm,),
        in_specs=[
            pl.BlockSpec((tm, D_MODEL), tok),
            pl.BlockSpec((tm, V_WIDTH), tok),
            pl.BlockSpec((tm, D_MODEL), tok),
            pl.BlockSpec((tm, D_MODEL), tok),
            _const_spec((V_WIDTH, D_MODEL)),
            _const_spec((D_MODEL, D_MODEL)),
        ] + _FFN_WEIGHT_SPECS() + [_const_spec((1, D_MODEL))],
        out_specs=pl.BlockSpec((tm, D_MODEL), tok),
        out_shape=jax.ShapeDtypeStruct((t, D_MODEL), F32),
        name="merge_ffn_final",
        compiler_params=pltpu.CompilerParams(
            dimension_semantics=("arbitrary",), vmem_limit_bytes=V7X_VMEM_LIMIT_BYTES),
    )(h2d, o2d, gy2d, ga2d, wao, wo, g, wg, wu, wd, gf)


def _rope(x, cos, sin_signed, lo_half):
    outs = []
    for c in range(x.shape[1] // 128):
        xc = x[:, c * 128:(c + 1) * 128]
        rot = jnp.where(lo_half, pltpu.roll(xc, 96, 1), pltpu.roll(xc, 32, 1))
        outs.append(xc * cos + rot * sin_signed)
    return jnp.concatenate(outs, axis=1)


def _in_proj_kernel(h_ref, g_ref, w_ref, bgate_ref, convw_ref, wco_ref, cosq_ref, sinq_ref,
                    cosk_ref, sink_ref, gy_ref, ga_ref, q_ref, k_ref, vt_ref, zbuf_ref):
    tm = h_ref.shape[1]

    @pl.when(pl.program_id(1) == 0)
    def _():
        zbuf_ref[0:CONV_HALO, :] = jnp.zeros((CONV_HALO, CONV_WIDTH), F32)

    u = _rms(h_ref[0], g_ref[...]).astype(BF16)

    def proj(off, width):
        return _mm(u, w_ref[:, off:off + width])

    vt_ref[0] = proj(OFF_V, V_WIDTH).T.astype(BF16)
    lane = lax.broadcasted_iota(jnp.int32, (tm, 128), 1)
    lo_half = (lane % DA_HEAD_DIM) < (DA_HEAD_DIM // 2)
    q_ref[0] = _rope(proj(OFF_Q, QK_WIDTH), cosq_ref[...], sinq_ref[...], lo_half).astype(BF16)
    k_ref[0] = _rope(proj(OFF_K, QK_WIDTH), cosk_ref[...], sink_ref[...], lo_half).astype(BF16)

    z = proj(OFF_CG, CONV_WIDTH) * proj(OFF_XC, CONV_WIDTH)
    zbuf_ref[CONV_HALO:CONV_HALO + tm, :] = z
    cw = convw_ref[...]
    zc = (cw[0:1, :] * zbuf_ref[CONV_HALO - 2:CONV_HALO - 2 + tm, :]
          + cw[1:2, :] * zbuf_ref[CONV_HALO - 1:CONV_HALO - 1 + tm, :]
          + cw[2:3, :] * z)
    zbuf_ref[0:CONV_HALO, :] = z[tm - CONV_HALO:, :]
    ya = _mm((proj(OFF_BG, CONV_WIDTH) * zc).astype(BF16), wco_ref[...])
    g_conv = _sigmoid(proj(OFF_GC, D_MODEL) + bgate_ref[:, 0:D_MODEL])
    gy_ref[0] = (g_conv * ya).astype(gy_ref.dtype)
    ga_ref[0] = _sigmoid(proj(OFF_GA, D_MODEL) + bgate_ref[:, D_MODEL:]).astype(ga_ref.dtype)


def _in_proj(h3d, g, w_in, b_gate, conv_w, w_conv_out, cosq, sinq, cosk, sink):
    b, s, _ = h3d.shape
    tm = TM_PROJ
    tok = lambda bi, si: (bi, si, 0)
    tab = lambda bi, si: (si, 0)
    bsd = lambda width, dt: jax.ShapeDtypeStruct((b, s, width), dt)
    return pl.pallas_call(
        _in_proj_kernel,
        grid=(b, s // tm),
        in_specs=[
            pl.BlockSpec((1, tm, D_MODEL), tok),
            _const_spec((1, D_MODEL)),
            _const_spec((D_MODEL, IN_WIDTH)),
            _const_spec((1, 2 * D_MODEL)),
            _const_spec((CONV_K, CONV_WIDTH)),
            _const_spec((CONV_WIDTH, D_MODEL)),
            pl.BlockSpec((tm, 128), tab),
            pl.BlockSpec((tm, 128), tab),
            pl.BlockSpec((tm, 128), tab),
            pl.BlockSpec((tm, 128), tab),
        ],
        out_specs=[
            pl.BlockSpec((1, tm, D_MODEL), tok),
            pl.BlockSpec((1, tm, D_MODEL), tok),
            pl.BlockSpec((1, tm, QK_WIDTH), tok),
            pl.BlockSpec((1, tm, QK_WIDTH), tok),
            pl.BlockSpec((1, V_WIDTH, tm), lambda bi, si: (bi, 0, si)),
        ],
        out_shape=[bsd(D_MODEL, BF16), bsd(D_MODEL, BF16), bsd(QK_WIDTH, BF16),
                   bsd(QK_WIDTH, BF16), jax.ShapeDtypeStruct((b, V_WIDTH, s), BF16)],
        scratch_shapes=[pltpu.VMEM((CONV_HALO + tm, CONV_WIDTH), F32)],
        name="in_proj",
        compiler_params=pltpu.CompilerParams(
            dimension_semantics=("arbitrary", "arbitrary"),
            vmem_limit_bytes=V7X_VMEM_LIMIT_BYTES),
    )(h3d, g, w_in, b_gate, conv_w, w_conv_out, cosq, sinq, cosk, sink)


def _attn_kernel(q_ref, k_ref, vt_ref, bias_ref, lamv_ref, sg_ref, o_ref,
                 acc_ref, sa_ref, sb_ref):
    tq = TQ
    n_qtiles = q_ref.shape[1] // tq
    heads = range(HEADS_PER_STEP)
    hcols = lambda hh: slice(hh * 128, (hh + 1) * 128)
    qrows = lambda qi: pl.ds(pl.multiple_of(qi * tq, tq), tq)

    feat = lax.broadcasted_iota(jnp.int32, (128, tq), 0)
    zero = jnp.zeros((128, tq), F32)
    ones = jnp.ones((BF16_SUBLANES, TK), BF16)
    lv = lamv_ref[...]
    lam = (jnp.exp(jnp.sum(lv[0:1] * lv[1:2], axis=-1, keepdims=True))
           - jnp.exp(jnp.sum(lv[2:3] * lv[3:4], axis=-1, keepdims=True)) + LAMBDA_INIT)
    out_gain = sg_ref[...] * (1.0 - LAMBDA_INIT)

    def stacked_qt(qi, hh):
        qt = q_ref[0, qrows(qi), hcols(hh)].astype(F32).T
        return jnp.concatenate([jnp.where(feat < DA_HEAD_DIM, qt, zero),
                                jnp.where(feat >= DA_HEAD_DIM, qt, zero)], axis=1).astype(BF16)

    def scores(qqt, hh, tile, s_ref, masked):
        kt = k_ref[0, pl.ds(pl.multiple_of(tile * TK, TK), TK), hcols(hh)]
        s = _mm(kt, qqt)
        if masked:
            s = s + bias_ref[...]
        s_ref[...] = s
        return jnp.max(s, axis=0, keepdims=True)

    def accumulate(hh, tile, s_ref, s_max, m):
        start = pl.multiple_of(tile * TK, TK)
        vt = jnp.concatenate([vt_ref[0, hcols(hh), pl.ds(start, TK)], ones], axis=0)
        m_new = jnp.maximum(m, s_max)
        p = jnp.exp2((s_ref[...] - m_new).astype(BF16))
        acc_ref[hh] = jnp.exp2(m - m_new) * acc_ref[hh] + _mm(vt, p)
        return m_new

    def finalize(qi, hh):
        o = acc_ref[hh, 0:DA_V_DIM, :] / acc_ref[hh, DA_V_DIM:DA_V_DIM + 1, :]
        o = o[:, :tq] - lam * o[:, tq:]
        o = o * lax.rsqrt(jnp.mean(o * o, axis=0, keepdims=True) + NORM_EPS) * out_gain
        o_ref[0, qrows(qi), hcols(hh)] = o.T.astype(o_ref.dtype)

    def q_tile(qi, sa_max):
        qqt = [stacked_qt(qi, hh) for hh in heads]
        acc_ref[...] = jnp.zeros_like(acc_ref)

        def body(j, carry):
            ma, mb, sa_max = carry
            sb_max = scores(qqt[1], 1, j, sb_ref, False)
            ma = accumulate(0, jnp.where(j == 0, qi, j - 1), sa_ref, sa_max, ma)
            sa_max = scores(qqt[0], 0, j, sa_ref, False)
            mb = accumulate(1, j, sb_ref, sb_max, mb)
            return ma, mb, sa_max

        m0 = jnp.full((1, 2 * tq), -jnp.inf, F32)
        carry = lax.fori_loop(0, qi // 2, lambda i, c: body(2 * i + 1, body(2 * i, c)),
                              (m0, m0, sa_max))
        ma, mb, sa_max = lax.cond(qi % 2 == 1, lambda c: body(qi - 1, c), lambda c: c, carry)
        sb_max = scores(qqt[1], 1, qi, sb_ref, True)
        accumulate(0, jnp.maximum(qi - 1, 0), sa_ref, sa_max, ma)
        nxt = jnp.minimum(qi + 1, n_qtiles - 1)
        sa_max = scores(stacked_qt(nxt, 0), 0, nxt, sa_ref, True)
        accumulate(1, qi, sb_ref, sb_max, mb)
        for hh in heads:
            finalize(qi, hh)
        return sa_max

    lax.fori_loop(0, n_qtiles, q_tile, scores(stacked_qt(0, 0), 0, 0, sa_ref, True))


def _diff_attn(q, k, vt, bias, lamv, subln_g_col):
    b, s, _ = q.shape
    hw = 128 * HEADS_PER_STEP
    return pl.pallas_call(
        _attn_kernel,
        grid=(b, DA_HEADS // HEADS_PER_STEP),
        in_specs=[
            pl.BlockSpec((1, s, hw), lambda bi, hi: (bi, 0, hi)),
            pl.BlockSpec((1, s, hw), lambda bi, hi: (bi, 0, hi)),
            pl.BlockSpec((1, hw, s), lambda bi, hi: (bi, hi, 0)),
            _const_spec((TK, 2 * TQ)),
            _const_spec((4, DA_HEAD_DIM)),
            _const_spec((DA_V_DIM, 1)),
        ],
        out_specs=pl.BlockSpec((1, s, hw), lambda bi, hi: (bi, 0, hi)),
        out_shape=jax.ShapeDtypeStruct((b, s, V_WIDTH), BF16),
        scratch_shapes=[pltpu.VMEM((HEADS_PER_STEP, DA_V_DIM + BF16_SUBLANES, 2 * TQ), F32),
                        pltpu.VMEM((TK, 2 * TQ), F32),
                        pltpu.VMEM((TK, 2 * TQ), F32)],
        name="diff_attn",
        compiler_params=pltpu.CompilerParams(
            dimension_semantics=("arbitrary", "arbitrary"),
            vmem_limit_bytes=V7X_VMEM_LIMIT_BYTES),
    )(q, k, vt, bias, lamv, subln_g_col)


def _rope_tables(seq):
    half = DA_HEAD_DIM // 2
    inv_freq = 1.0 / (ROPE_THETA ** (jnp.arange(0, DA_HEAD_DIM, 2, dtype=F32) / DA_HEAD_DIM))
    ang = jnp.arange(seq, dtype=F32)[:, None] * inv_freq[None, :]
    cos, sin = jnp.cos(ang), jnp.sin(ang)
    cos128 = jnp.tile(cos, (1, 128 // half))
    sin128 = jnp.tile(jnp.concatenate([-sin, sin], axis=1), (1, 128 // DA_HEAD_DIM))
    return cos128, sin128


def kernel(x, norm_ffn1, ffn1_gate, ffn1_up, ffn1_down, norm_mix, w_in, b_gate, conv_w,
           w_conv_out, lambda_q1, lambda_k1, lambda_q2, lambda_k2, subln_g, w_attn_out, w_o,
           norm_ffn2, ffn2_gate, ffn2_up, ffn2_down, norm_final):
    bsz, seq, d = x.shape
    assert d == D_MODEL and norm_ffn1.shape[0] == 1
    assert seq % TQ == 0 and seq % TM_PROJ == 0
    assert (bsz * seq) % TM_FFN == 0 and (bsz * seq) % TM_MERGE_FFN == 0
    assert TQ == TK and TQ % CHUNK == 0
    assert HEADS_PER_STEP == 2 and DA_HEADS % HEADS_PER_STEP == 0
    t = bsz * seq
    l = 0
    bf = lambda a: a.astype(BF16)
    row = lambda a: a.reshape(1, -1)

    cos128, sin128 = _rope_tables(seq)
    scale = math.log2(math.e) / math.sqrt(DA_HEAD_DIM)
    pos_chunk = jnp.arange(TQ, dtype=jnp.int32) // CHUNK
    bias = jnp.where(pos_chunk[:, None] <= pos_chunk[None, :], 0.0, -jnp.inf).astype(F32)
    bias = jnp.tile(bias, (1, 2))
    lamv = jnp.stack([lambda_q1[l], lambda_k1[l], lambda_q2[l], lambda_k2[l]]).astype(F32)

    h = _ffn(x.reshape(t, d), row(norm_ffn1[l]), bf(ffn1_gate[l]), bf(ffn1_up[l]),
             bf(ffn1_down[l]))
    gy, ga, q, k, vt = _in_proj(
        h.reshape(bsz, seq, d), row(norm_mix[l]), bf(w_in[l]), row(b_gate[l]), conv_w[l],
        bf(w_conv_out[l]), cos128 * scale, sin128 * scale, cos128, sin128)
    o = _diff_attn(q, k, vt, bias, lamv, subln_g[l].reshape(-1, 1))
    out = _merge_ffn(h, o.reshape(t, V_WIDTH), gy.reshape(t, d), ga.reshape(t, d),
                     bf(w_attn_out[l]), bf(w_o[l]), row(norm_ffn2[l]), bf(ffn2_gate[l]),
                     bf(ffn2_up[l]), bf(ffn2_down[l]), row(norm_final))
    return out.reshape(bsz, seq, d)
```

```python
import math

import jax
import jax.numpy as jnp
from jax import lax
from jax.experimental import pallas as pl
from jax.experimental.pallas import tpu as pltpu

D_MODEL = 1024
CHUNK = 64
CONV_WIDTH = 512
CONV_K = 3
DA_HEADS = 4
DA_HEAD_DIM = 64
DA_V_DIM = 128
QK_WIDTH = 512
V_WIDTH = 512
D_FF = 2816
ROPE_THETA = 10000.0
NORM_EPS = 1e-6
LAMBDA_INIT = 0.8 - 0.6 * math.exp(-0.3 * 0)

OFF_XC, OFF_BG, OFF_CG, OFF_Q, OFF_K, OFF_V, OFF_GC, OFF_GA = (
    0, 512, 1024, 1536, 2048, 2560, 3072, 4096)
IN_WIDTH = 5120

V7X_VMEM_LIMIT_BYTES = 56 * 1024 * 1024
BF16_SUBLANES = 16

TM_FFN = 1024
TM_MERGE_FFN = 512
TM_PROJ = 1024
TQ = 512
TK = 512
HEADS_PER_STEP = 2
ATTN_COL_BLOCK = 512
CONV_HALO = 8

BF16 = jnp.bfloat16
F32 = jnp.float32


def _const_spec(shape):
    return pl.BlockSpec(shape, lambda *_: (0,) * len(shape), pipeline_mode=pl.Buffered(1))


def _rms(x, g):
    return x * lax.rsqrt(jnp.mean(x * x, axis=-1, keepdims=True) + NORM_EPS) * g


def _sigmoid(x):
    return 0.5 * jnp.tanh(0.5 * x) + 0.5


def _mm(a, b):
    return jnp.dot(a, b, preferred_element_type=F32)


def _mm_nt(a, b):
    return lax.dot_general(a, b, (((1,), (1,)), ((), ())), preferred_element_type=F32)


def _half_swiglu_step(x, g_ref, wg_ref, wu_ref, wd_ref):
    xn = _rms(x, g_ref[...]).astype(BF16)
    gate = _mm(xn, wg_ref[...])
    up = _mm(xn, wu_ref[...])
    a = (gate * _sigmoid(gate) * up).astype(BF16)
    return x + 0.5 * _mm(a, wd_ref[...])


def _ffn_kernel(x_ref, g_ref, wg_ref, wu_ref, wd_ref, o_ref):
    o_ref[...] = _half_swiglu_step(x_ref[...], g_ref, wg_ref, wu_ref, wd_ref)


def _merge_ffn_kernel(h_ref, o_ref, gy_ref, ga_ref, wao_ref, wo_ref, g_ref, wg_ref, wu_ref,
                      wd_ref, gf_ref, out_ref):
    y_attn = _mm(o_ref[...], wao_ref[...])
    mix = gy_ref[...].astype(F32) + ga_ref[...].astype(F32) * y_attn
    h2 = h_ref[...] + _mm(mix.astype(BF16), wo_ref[...])
    out_ref[...] = _rms(_half_swiglu_step(h2, g_ref, wg_ref, wu_ref, wd_ref), gf_ref[...])


_FFN_WEIGHT_SPECS = lambda: [
    _const_spec((1, D_MODEL)),
    _const_spec((D_MODEL, D_FF)),
    _const_spec((D_MODEL, D_FF)),
    _const_spec((D_FF, D_MODEL)),
]


def _ffn(x2d, g, wg, wu, wd):
    t = x2d.shape[0]
    tok = lambda i: (i, 0)
    return pl.pallas_call(
        _ffn_kernel,
        grid=(t // TM_FFN,),
        in_specs=[pl.BlockSpec((TM_FFN, D_MODEL), tok)] + _FFN_WEIGHT_SPECS(),
        out_specs=pl.BlockSpec((TM_FFN, D_MODEL), tok),
        out_shape=jax.ShapeDtypeStruct((t, D_MODEL), F32),
        name="ffn",
        compiler_params=pltpu.CompilerParams(
            dimension_semantics=("arbitrary",), vmem_limit_bytes=V7X_VMEM_LIMIT_BYTES),
    )(x2d, g, wg, wu, wd)


def _merge_ffn(h2d, o2d, gy2d, ga2d, wao, wo, g, wg, wu, wd, gf):
    t = h2d.shape[0]
    tm = TM_MERGE_FFN
    tok = lambda i: (i, 0)
    return pl.pallas_call(
        _merge_ffn_kernel,
        grid=(t // tm,),
        in_specs=[
            pl.BlockSpec((tm, D_MODEL), tok),
            pl.BlockSpec((tm, V_WIDTH), tok),
            pl.BlockSpec((tm, D_MODEL), tok),
            pl.BlockSpec((tm, D_MODEL), tok),
            _const_spec((V_WIDTH, D_MODEL)),
            _const_spec((D_MODEL, D_MODEL)),
        ] + _FFN_WEIGHT_SPECS() + [_const_spec((1, D_MODEL))],
        out_specs=pl.BlockSpec((tm, D_MODEL), tok),
        out_shape=jax.ShapeDtypeStruct((t, D_MODEL), F32),
        name="merge_ffn_final",
        compiler_params=pltpu.CompilerParams(
            dimension_semantics=("arbitrary",), vmem_limit_bytes=V7X_VMEM_LIMIT_BYTES),
    )(h2d, o2d, gy2d, ga2d, wao, wo, g, wg, wu, wd, gf)


def _rope(x, cos, sin_signed, lo_half):
    outs = []
    for c in range(x.shape[1] // 128):
        xc = x[:, c * 128:(c + 1) * 128]
        rot = jnp.where(lo_half, pltpu.roll(xc, 96, 1), pltpu.roll(xc, 32, 1))
        outs.append(xc * cos + rot * sin_signed)
    return jnp.concatenate(outs, axis=1)


def _in_proj_kernel(h_ref, g_ref, w_ref, bgate_ref, convw_ref, wco_ref, cosq_ref, sinq_ref,
                    cosk_ref, sink_ref, gy_ref, ga_ref, q_ref, k_ref, vt_ref, zbuf_ref):
    tm = h_ref.shape[1]

    @pl.when(pl.program_id(1) == 0)
    def _():
        zbuf_ref[0:CONV_HALO, :] = jnp.zeros((CONV_HALO, CONV_WIDTH), F32)

    u = _rms(h_ref[0], g_ref[...]).astype(BF16)

    def proj(off, width):
        return _mm(u, w_ref[:, off:off + width])

    vt_ref[0] = proj(OFF_V, V_WIDTH).T.astype(BF16)
    lane = lax.broadcasted_iota(jnp.int32, (tm, 128), 1)
    lo_half = (lane % DA_HEAD_DIM) < (DA_HEAD_DIM // 2)
    q_ref[0] = _rope(proj(OFF_Q, QK_WIDTH), cosq_ref[...], sinq_ref[...], lo_half).astype(BF16)
    k_ref[0] = _rope(proj(OFF_K, QK_WIDTH), cosk_ref[...], sink_ref[...], lo_half).astype(BF16)

    z = proj(OFF_CG, CONV_WIDTH) * proj(OFF_XC, CONV_WIDTH)
    zbuf_ref[CONV_HALO:CONV_HALO + tm, :] = z
    cw = convw_ref[...]
    zc = (cw[0:1, :] * zbuf_ref[CONV_HALO - 2:CONV_HALO - 2 + tm, :]
          + cw[1:2, :] * zbuf_ref[CONV_HALO - 1:CONV_HALO - 1 + tm, :]
          + cw[2:3, :] * z)
    zbuf_ref[0:CONV_HALO, :] = z[tm - CONV_HALO:, :]
    ya = _mm((proj(OFF_BG, CONV_WIDTH) * zc).astype(BF16), wco_ref[...])
    g_conv = _sigmoid(proj(OFF_GC, D_MODEL) + bgate_ref[:, 0:D_MODEL])
    gy_ref[0] = (g_conv * ya).astype(gy_ref.dtype)
    ga_ref[0] = _sigmoid(proj(OFF_GA, D_MODEL) + bgate_ref[:, D_MODEL:]).astype(ga_ref.dtype)


def _in_proj(h3d, g, w_in, b_gate, conv_w, w_conv_out, cosq, sinq, cosk, sink):
    b, s, _ = h3d.shape
    tm = TM_PROJ
    tok = lambda bi, si: (bi, si, 0)
    tab = lambda bi, si: (si, 0)
    bsd = lambda width, dt: jax.ShapeDtypeStruct((b, s, width), dt)
    return pl.pallas_call(
        _in_proj_kernel,
        grid=(b, s // tm),
        in_specs=[
            pl.BlockSpec((1, tm, D_MODEL), tok),
            _const_spec((1, D_MODEL)),
            _const_spec((D_MODEL, IN_WIDTH)),
            _const_spec((1, 2 * D_MODEL)),
            _const_spec((CONV_K, CONV_WIDTH)),
            _const_spec((CONV_WIDTH, D_MODEL)),
            pl.BlockSpec((tm, 128), tab),
            pl.BlockSpec((tm, 128), tab),
            pl.BlockSpec((tm, 128), tab),
            pl.BlockSpec((tm, 128), tab),
        ],
        out_specs=[
            pl.BlockSpec((1, tm, D_MODEL), tok),
            pl.BlockSpec((1, tm, D_MODEL), tok),
            pl.BlockSpec((1, tm, QK_WIDTH), tok),
            pl.BlockSpec((1, tm, QK_WIDTH), tok),
            pl.BlockSpec((1, V_WIDTH, tm), lambda bi, si: (bi, 0, si)),
        ],
        out_shape=[bsd(D_MODEL, BF16), bsd(D_MODEL, BF16), bsd(QK_WIDTH, BF16),
                   bsd(QK_WIDTH, BF16), jax.ShapeDtypeStruct((b, V_WIDTH, s), BF16)],
        scratch_shapes=[pltpu.VMEM((CONV_HALO + tm, CONV_WIDTH), F32)],
        name="in_proj",
        compiler_params=pltpu.CompilerParams(
            dimension_semantics=("arbitrary", "arbitrary"),
            vmem_limit_bytes=V7X_VMEM_LIMIT_BYTES),
    )(h3d, g, w_in, b_gate, conv_w, w_conv_out, cosq, sinq, cosk, sink)


def _attn_kernel(q_ref, k_ref, vt_ref, bias_ref, lamv_ref, sg_ref, o_ref,
                 acc_ref, sa_ref, sb_ref):
    tq = TQ
    n_qtiles = q_ref.shape[1] // tq
    heads = range(HEADS_PER_STEP)
    hcols = lambda hh: slice(hh * 128, (hh + 1) * 128)
    qrows = lambda qi: pl.ds(pl.multiple_of(qi * tq, tq), tq)

    feat = lax.broadcasted_iota(jnp.int32, (128, tq), 0)
    zero = jnp.zeros((128, tq), F32)
    ones = jnp.ones((BF16_SUBLANES, TK), BF16)
    lv = lamv_ref[...]
    lam = (jnp.exp(jnp.sum(lv[0:1] * lv[1:2], axis=-1, keepdims=True))
           - jnp.exp(jnp.sum(lv[2:3] * lv[3:4], axis=-1, keepdims=True)) + LAMBDA_INIT)
    out_gain = sg_ref[...] * (1.0 - LAMBDA_INIT)

    def stacked_qt(qi, hh):
        qt = q_ref[0, qrows(qi), hcols(hh)].astype(F32).T
        return jnp.concatenate([jnp.where(feat < DA_HEAD_DIM, qt, zero),
                                jnp.where(feat >= DA_HEAD_DIM, qt, zero)], axis=1).astype(BF16)

    cb = ATTN_COL_BLOCK
    subs = range(2 * tq // cb)
    scols = lambda sub: slice(sub * cb, (sub + 1) * cb)

    def scores(qqt, hh, tile, s_ref, masked, sub):
        kt = k_ref[0, pl.ds(pl.multiple_of(tile * TK, TK), TK), hcols(hh)]
        s = _mm(kt, qqt[:, scols(sub)])
        if masked:
            q0 = (sub * cb) % tq
            s = s + bias_ref[:, q0:q0 + cb]
        s_ref[:, scols(sub)] = s
        return jnp.max(s, axis=0, keepdims=True)

    def accumulate(hh, tile, s_ref, s_max, m, sub):
        start = pl.multiple_of(tile * TK, TK)
        vt = jnp.concatenate([vt_ref[0, hcols(hh), pl.ds(start, TK)], ones], axis=0)
        m_new = jnp.maximum(m, s_max)
        p = jnp.exp2((s_ref[:, scols(sub)] - m_new).astype(BF16))
        acc_ref[hh, :, scols(sub)] = (jnp.exp2(m - m_new) * acc_ref[hh, :, scols(sub)]
                                      + _mm(vt, p))
        return m_new

    def phase(score_args, acc_args, s_maxes, ms):
        new_maxes, new_ms = [], []
        for sub in subs:
            if score_args is not None:
                new_maxes.append(scores(*score_args, sub))
            new_ms.append(accumulate(*acc_args, s_maxes[sub], ms[sub], sub))
        return tuple(new_maxes), tuple(new_ms)

    def finalize(qi, hh):
        o = acc_ref[hh, 0:DA_V_DIM, :] / acc_ref[hh, DA_V_DIM:DA_V_DIM + 1, :]
        o = o[:, :tq] - lam * o[:, tq:]
        o = o * lax.rsqrt(jnp.mean(o * o, axis=0, keepdims=True) + NORM_EPS) * out_gain
        o_ref[0, qrows(qi), hcols(hh)] = o.T.astype(o_ref.dtype)

    def q_tile(qi, sa_max):
        qqt = [stacked_qt(qi, hh) for hh in heads]
        acc_ref[...] = jnp.zeros_like(acc_ref)

        def body(j, carry):
            ma, mb, sa_max = carry
            sb_max, ma = phase((qqt[1], 1, j, sb_ref, False),
                               (0, jnp.where(j == 0, qi, j - 1), sa_ref), sa_max, ma)
            sa_max, mb = phase((qqt[0], 0, j, sa_ref, False), (1, j, sb_ref), sb_max, mb)
            return ma, mb, sa_max

        m0 = (jnp.full((1, cb), -jnp.inf, F32),) * len(subs)
        carry = lax.fori_loop(0, qi // 2, lambda i, c: body(2 * i + 1, body(2 * i, c)),
                              (m0, m0, sa_max))
        ma, mb, sa_max = lax.cond(qi % 2 == 1, lambda c: body(qi - 1, c), lambda c: c, carry)
        sb_max, _ = phase((qqt[1], 1, qi, sb_ref, True),
                          (0, jnp.maximum(qi - 1, 0), sa_ref), sa_max, ma)
        finalize(qi, 0)
        nxt = jnp.minimum(qi + 1, n_qtiles - 1)
        sa_max, _ = phase((stacked_qt(nxt, 0), 0, nxt, sa_ref, True), (1, qi, sb_ref), sb_max, mb)
        finalize(qi, 1)
        return sa_max

    qqt0 = stacked_qt(0, 0)
    lax.fori_loop(0, n_qtiles, q_tile,
                  tuple(scores(qqt0, 0, 0, sa_ref, True, sub) for sub in subs))


def _diff_attn(q, k, vt, bias, lamv, subln_g_col):
    b, s, _ = q.shape
    hw = 128 * HEADS_PER_STEP
    return pl.pallas_call(
        _attn_kernel,
        grid=(b, DA_HEADS // HEADS_PER_STEP),
        in_specs=[
            pl.BlockSpec((1, s, hw), lambda bi, hi: (bi, 0, hi)),
            pl.BlockSpec((1, s, hw), lambda bi, hi: (bi, 0, hi)),
            pl.BlockSpec((1, hw, s), lambda bi, hi: (bi, hi, 0)),
            _const_spec((TK, TQ)),
            _const_spec((4, DA_HEAD_DIM)),
            _const_spec((DA_V_DIM, 1)),
        ],
        out_specs=pl.BlockSpec((1, s, hw), lambda bi, hi: (bi, 0, hi)),
        out_shape=jax.ShapeDtypeStruct((b, s, V_WIDTH), BF16),
        scratch_shapes=[pltpu.VMEM((HEADS_PER_STEP, DA_V_DIM + BF16_SUBLANES, 2 * TQ), F32),
                        pltpu.VMEM((TK, 2 * TQ), F32),
                        pltpu.VMEM((TK, 2 * TQ), F32)],
        name="diff_attn",
        compiler_params=pltpu.CompilerParams(
            dimension_semantics=("arbitrary", "arbitrary"),
            vmem_limit_bytes=V7X_VMEM_LIMIT_BYTES),
    )(q, k, vt, bias, lamv, subln_g_col)


def _rope_tables(seq):
    half = DA_HEAD_DIM // 2
    inv_freq = 1.0 / (ROPE_THETA ** (jnp.arange(0, DA_HEAD_DIM, 2, dtype=F32) / DA_HEAD_DIM))
    ang = jnp.arange(seq, dtype=F32)[:, None] * inv_freq[None, :]
    cos, sin = jnp.cos(ang), jnp.sin(ang)
    cos128 = jnp.tile(cos, (1, 128 // half))
    sin128 = jnp.tile(jnp.concatenate([-sin, sin], axis=1), (1, 128 // DA_HEAD_DIM))
    return cos128, sin128


def kernel(x, norm_ffn1, ffn1_gate, ffn1_up, ffn1_down, norm_mix, w_in, b_gate, conv_w,
           w_conv_out, lambda_q1, lambda_k1, lambda_q2, lambda_k2, subln_g, w_attn_out, w_o,
           norm_ffn2, ffn2_gate, ffn2_up, ffn2_down, norm_final):
    bsz, seq, d = x.shape
    assert d == D_MODEL and norm_ffn1.shape[0] == 1
    assert seq % TQ == 0 and seq % TM_PROJ == 0
    assert (bsz * seq) % TM_FFN == 0 and (bsz * seq) % TM_MERGE_FFN == 0
    assert TQ == TK and TQ % CHUNK == 0
    assert HEADS_PER_STEP == 2 and DA_HEADS % HEADS_PER_STEP == 0
    t = bsz * seq
    l = 0
    bf = lambda a: a.astype(BF16)
    row = lambda a: a.reshape(1, -1)

    cos128, sin128 = _rope_tables(seq)
    scale = math.log2(math.e) / math.sqrt(DA_HEAD_DIM)
    pos_chunk = jnp.arange(TQ, dtype=jnp.int32) // CHUNK
    bias = jnp.where(pos_chunk[:, None] <= pos_chunk[None, :], 0.0, -jnp.inf).astype(F32)
    lamv = jnp.stack([lambda_q1[l], lambda_k1[l], lambda_q2[l], lambda_k2[l]]).astype(F32)

    h = _ffn(x.reshape(t, d), row(norm_ffn1[l]), bf(ffn1_gate[l]), bf(ffn1_up[l]),
             bf(ffn1_down[l]))
    gy, ga, q, k, vt = _in_proj(
        h.reshape(bsz, seq, d), row(norm_mix[l]), bf(w_in[l]), row(b_gate[l]), conv_w[l],
        bf(w_conv_out[l]), cos128 * scale, sin128 * scale, cos128, sin128)
    o = _diff_attn(q, k, vt, bias, lamv, subln_g[l].reshape(-1, 1))
    out = _merge_ffn(h, o.reshape(t, V_WIDTH), gy.reshape(t, d), ga.reshape(t, d),
                     bf(w_attn_out[l]), bf(w_o[l]), row(norm_ffn2[l]), bf(ffn2_gate[l]),
                     bf(ffn2_up[l]), bf(ffn2_down[l]), row(norm_final))
    return out.reshape(bsz, seq, d)
```

```python
import math

import jax
import jax.numpy as jnp
from jax import lax
from jax.experimental import pallas as pl
from jax.experimental.pallas import tpu as pltpu

D_MODEL = 1024
CHUNK = 64
CONV_WIDTH = 512
CONV_K = 3
DA_HEADS = 4
DA_HEAD_DIM = 64
DA_V_DIM = 128
QK_WIDTH = 512
V_WIDTH = 512
D_FF = 2816
ROPE_THETA = 10000.0
NORM_EPS = 1e-6
LAMBDA_INIT = 0.8 - 0.6 * math.exp(-0.3 * 0)

OFF_XC, OFF_BG, OFF_CG, OFF_Q, OFF_K, OFF_V, OFF_GC, OFF_GA = (
    0, 512, 1024, 1536, 2048, 2560, 3072, 4096)
IN_WIDTH = 5120

V7X_VMEM_LIMIT_BYTES = 56 * 1024 * 1024
BF16_SUBLANES = 16

TM_FFN = 1024
TM_MERGE_FFN = 512
TM_PROJ = 1024
TQ = 512
TK = 512
HEADS_PER_STEP = 2
ATTN_COL_BLOCK = 512
CONV_HALO = 8

BF16 = jnp.bfloat16
F32 = jnp.float32


def _const_spec(shape):
    return pl.BlockSpec(shape, lambda *_: (0,) * len(shape), pipeline_mode=pl.Buffered(1))


def _rms(x, g):
    return x * lax.rsqrt(jnp.mean(x * x, axis=-1, keepdims=True) + NORM_EPS) * g


def _sigmoid(x):
    return 0.5 * jnp.tanh(0.5 * x) + 0.5


def _mm(a, b):
    return jnp.dot(a, b, preferred_element_type=F32)


def _mm_nt(a, b):
    return lax.dot_general(a, b, (((1,), (1,)), ((), ())), preferred_element_type=F32)


def _ffn_kernel(x_ref, g_ref, wg_ref, wu_ref, wd_ref, o_ref):
    x = x_ref[...]
    xn = _rms(x, g_ref[...]).astype(BF16)
    gate = _mm(xn, wg_ref[...])
    up = _mm(xn, wu_ref[...])
    a = (gate * _sigmoid(gate) * up).astype(BF16)
    o_ref[...] = x + 0.5 * _mm(a, wd_ref[...])


def _merge_ffn_kernel(h_ref, o_ref, gy_ref, ga_ref, wao_ref, wo_ref, g_ref, wg_ref, wu_ref,
                      wd_ref, gf_ref, out_ref):
    half = h_ref.shape[0] // 2
    rows = [slice(0, half), slice(half, 2 * half)]

    def merge(r):
        y_attn = _mm(o_ref[r, :], wao_ref[...])
        mix = gy_ref[r, :].astype(F32) + ga_ref[r, :].astype(F32) * y_attn
        return h_ref[r, :] + _mm(mix.astype(BF16), wo_ref[...])

    def gate_up(xn):
        return _mm(xn, wg_ref[...]), _mm(xn, wu_ref[...])

    def down(h2, gate, up):
        a = (gate * _sigmoid(gate) * up).astype(BF16)
        return h2 + 0.5 * _mm(a, wd_ref[...])

    norm = lambda x: _rms(x, g_ref[...]).astype(BF16)
    h2a = merge(rows[0])
    h2b = merge(rows[1])
    gu_a = gate_up(norm(h2a))
    ya = down(h2a, *gu_a)
    gu_b = gate_up(norm(h2b))
    out_ref[rows[0], :] = _rms(ya, gf_ref[...])
    yb = down(h2b, *gu_b)
    out_ref[rows[1], :] = _rms(yb, gf_ref[...])


_FFN_WEIGHT_SPECS = lambda: [
    _const_spec((1, D_MODEL)),
    _const_spec((D_MODEL, D_FF)),
    _const_spec((D_MODEL, D_FF)),
    _const_spec((D_FF, D_MODEL)),
]


def _ffn(x2d, g, wg, wu, wd):
    t = x2d.shape[0]
    tok = lambda i: (i, 0)
    return pl.pallas_call(
        _ffn_kernel,
        grid=(t // TM_FFN,),
        in_specs=[pl.BlockSpec((TM_FFN, D_MODEL), tok)] + _FFN_WEIGHT_SPECS(),
        out_specs=pl.BlockSpec((TM_FFN, D_MODEL), tok),
        out_shape=jax.ShapeDtypeStruct((t, D_MODEL), F32),
        name="ffn",
        compiler_params=pltpu.CompilerParams(
            dimension_semantics=("arbitrary",), vmem_limit_bytes=V7X_VMEM_LIMIT_BYTES),
    )(x2d, g, wg, wu, wd)


def _merge_ffn(h2d, o2d, gy2d, ga2d, wao, wo, g, wg, wu, wd, gf):
    t = h2d.shape[0]
    tm = TM_MERGE_FFN
    tok = lambda i: (i, 0)
    return pl.pallas_call(
        _merge_ffn_kernel,
        grid=(t // tm,),
        in_specs=[
            pl.BlockSpec((tm, D_MODEL), tok),
            pl.BlockSpec((tm, V_WIDTH), tok),
            pl.BlockSpec((tm, D_MODEL), tok),
            pl.BlockSpec((tm, D_MODEL), tok),
            _const_spec((V_WIDTH, D_MODEL)),
            _const_spec((D_MODEL, D_MODEL)),
        ] + _FFN_WEIGHT_SPECS() + [_const_spec((1, D_MODEL))],
        out_specs=pl.BlockSpec((tm, D_MODEL), tok),
        out_shape=jax.ShapeDtypeStruct((t, D_MODEL), F32),
        name="merge_ffn_final",
        compiler_params=pltpu.CompilerParams(
            dimension_semantics=("arbitrary",), vmem_limit_bytes=V7X_VMEM_LIMIT_BYTES),
    )(h2d, o2d, gy2d, ga2d, wao, wo, g, wg, wu, wd, gf)


def _rope(x, cos, sin_signed, lo_half):
    outs = []
    for c in range(x.shape[1] // 128):
        xc = x[:, c * 128:(c + 1) * 128]
        rot = jnp.where(lo_half, pltpu.roll(xc, 96, 1), pltpu.roll(xc, 32, 1))
        outs.append(xc * cos + rot * sin_signed)
    return jnp.concatenate(outs, axis=1)


def _in_proj_kernel(h_ref, g_ref, w_ref, bgate_ref, convw_ref, wco_ref, cosq_ref, sinq_ref,
                    cosk_ref, sink_ref, gy_ref, ga_ref, q_ref, k_ref, vt_ref, zbuf_ref):
    tm = h_ref.shape[1]

    @pl.when(pl.program_id(1) == 0)
    def _():
        zbuf_ref[0:CONV_HALO, :] = jnp.zeros((CONV_HALO, CONV_WIDTH), F32)

    u = _rms(h_ref[0], g_ref[...]).astype(BF16)

    def proj(off, width):
        return _mm(u, w_ref[:, off:off + width])

    vt_ref[0] = proj(OFF_V, V_WIDTH).T.astype(BF16)
    lane = lax.broadcasted_iota(jnp.int32, (tm, 128), 1)
    lo_half = (lane % DA_HEAD_DIM) < (DA_HEAD_DIM // 2)
    q_ref[0] = _rope(proj(OFF_Q, QK_WIDTH), cosq_ref[...], sinq_ref[...], lo_half).astype(BF16)
    k_ref[0] = _rope(proj(OFF_K, QK_WIDTH), cosk_ref[...], sink_ref[...], lo_half).astype(BF16)

    z = proj(OFF_CG, CONV_WIDTH) * proj(OFF_XC, CONV_WIDTH)
    zbuf_ref[CONV_HALO:CONV_HALO + tm, :] = z
    cw = convw_ref[...]
    zc = (cw[0:1, :] * zbuf_ref[CONV_HALO - 2:CONV_HALO - 2 + tm, :]
          + cw[1:2, :] * zbuf_ref[CONV_HALO - 1:CONV_HALO - 1 + tm, :]
          + cw[2:3, :] * z)
    zbuf_ref[0:CONV_HALO, :] = z[tm - CONV_HALO:, :]
    ya = _mm((proj(OFF_BG, CONV_WIDTH) * zc).astype(BF16), wco_ref[...])
    g_conv = _sigmoid(proj(OFF_GC, D_MODEL) + bgate_ref[:, 0:D_MODEL])
    gy_ref[0] = (g_conv * ya).astype(gy_ref.dtype)
    ga_ref[0] = _sigmoid(proj(OFF_GA, D_MODEL) + bgate_ref[:, D_MODEL:]).astype(ga_ref.dtype)


def _in_proj(h3d, g, w_in, b_gate, conv_w, w_conv_out, cosq, sinq, cosk, sink):
    b, s, _ = h3d.shape
    tm = TM_PROJ
    tok = lambda bi, si: (bi, si, 0)
    tab = lambda bi, si: (si, 0)
    bsd = lambda width, dt: jax.ShapeDtypeStruct((b, s, width), dt)
    return pl.pallas_call(
        _in_proj_kernel,
        grid=(b, s // tm),
        in_specs=[
            pl.BlockSpec((1, tm, D_MODEL), tok),
            _const_spec((1, D_MODEL)),
            _const_spec((D_MODEL, IN_WIDTH)),
            _const_spec((1, 2 * D_MODEL)),
            _const_spec((CONV_K, CONV_WIDTH)),
            _const_spec((CONV_WIDTH, D_MODEL)),
            pl.BlockSpec((tm, 128), tab),
            pl.BlockSpec((tm, 128), tab),
            pl.BlockSpec((tm, 128), tab),
            pl.BlockSpec((tm, 128), tab),
        ],
        out_specs=[
            pl.BlockSpec((1, tm, D_MODEL), tok),
            pl.BlockSpec((1, tm, D_MODEL), tok),
            pl.BlockSpec((1, tm, QK_WIDTH), tok),
            pl.BlockSpec((1, tm, QK_WIDTH), tok),
            pl.BlockSpec((1, V_WIDTH, tm), lambda bi, si: (bi, 0, si)),
        ],
        out_shape=[bsd(D_MODEL, BF16), bsd(D_MODEL, BF16), bsd(QK_WIDTH, BF16),
                   bsd(QK_WIDTH, BF16), jax.ShapeDtypeStruct((b, V_WIDTH, s), BF16)],
        scratch_shapes=[pltpu.VMEM((CONV_HALO + tm, CONV_WIDTH), F32)],
        name="in_proj",
        compiler_params=pltpu.CompilerParams(
            dimension_semantics=("arbitrary", "arbitrary"),
            vmem_limit_bytes=V7X_VMEM_LIMIT_BYTES),
    )(h3d, g, w_in, b_gate, conv_w, w_conv_out, cosq, sinq, cosk, sink)


def _attn_kernel(q_ref, k_ref, vt_ref, bias_ref, lamv_ref, sg_ref, o_ref,
                 acc_ref, sa_ref, sb_ref, qqt_ref):
    tq = TQ
    n_qtiles = q_ref.shape[1] // tq
    heads = range(HEADS_PER_STEP)
    hcols = lambda hh: slice(hh * 128, (hh + 1) * 128)
    qrows = lambda qi: pl.ds(pl.multiple_of(qi * tq, tq), tq)

    feat = lax.broadcasted_iota(jnp.int32, (128, tq), 0)
    zero = jnp.zeros((128, tq), F32)
    ones = jnp.ones((BF16_SUBLANES, TK), BF16)
    lv = lamv_ref[...]
    lam = (jnp.exp(jnp.sum(lv[0:1] * lv[1:2], axis=-1, keepdims=True))
           - jnp.exp(jnp.sum(lv[2:3] * lv[3:4], axis=-1, keepdims=True)) + LAMBDA_INIT)
    out_gain = sg_ref[...] * (1.0 - LAMBDA_INIT)

    def stacked_qt(qi, hh):
        qt = q_ref[0, qrows(qi), hcols(hh)].astype(F32).T
        return jnp.concatenate([jnp.where(feat < DA_HEAD_DIM, qt, zero),
                                jnp.where(feat >= DA_HEAD_DIM, qt, zero)], axis=1).astype(BF16)

    cb = ATTN_COL_BLOCK
    subs = range(2 * tq // cb)
    scols = lambda sub: slice(sub * cb, (sub + 1) * cb)

    def scores(hh, tile, s_ref, masked, sub):
        kt = k_ref[0, pl.ds(pl.multiple_of(tile * TK, TK), TK), hcols(hh)]
        s = _mm(kt, qqt_ref[hh, :, scols(sub)])
        if masked:
            q0 = (sub * cb) % tq
            s = s + bias_ref[:, q0:q0 + cb]
        s_ref[:, scols(sub)] = s
        return jnp.max(s, axis=0, keepdims=True)

    def accumulate(hh, tile, s_ref, s_max, m, sub):
        start = pl.multiple_of(tile * TK, TK)
        vt = jnp.concatenate([vt_ref[0, hcols(hh), pl.ds(start, TK)], ones], axis=0)
        m_new = jnp.maximum(m, s_max)
        p = jnp.exp2((s_ref[:, scols(sub)] - m_new).astype(BF16))
        acc_ref[hh, :, scols(sub)] = (jnp.exp2(m - m_new) * acc_ref[hh, :, scols(sub)]
                                      + _mm(vt, p))
        return m_new

    def phase(score_args, acc_args, s_maxes, ms):
        new_maxes, new_ms = [], []
        for sub in subs:
            if score_args is not None:
                new_maxes.append(scores(*score_args, sub))
            new_ms.append(accumulate(*acc_args, s_maxes[sub], ms[sub], sub))
        return tuple(new_maxes), tuple(new_ms)

    def finalize(qi, hh):
        o = acc_ref[hh, 0:DA_V_DIM, :] * (1.0 / acc_ref[hh, DA_V_DIM:DA_V_DIM + 1, :])
        o = o[:, :tq] - lam * o[:, tq:]
        o = o * lax.rsqrt(jnp.mean(o * o, axis=0, keepdims=True) + NORM_EPS) * out_gain
        o_ref[0, qrows(qi), hcols(hh)] = o.T.astype(o_ref.dtype)
        acc_ref[hh] = jnp.zeros(acc_ref.shape[1:], F32)

    def q_tile(qi, sa_max):
        def body(j, carry):
            ma, mb, sa_max = carry
            sb_max, ma = phase((1, j, sb_ref, False),
                               (0, jnp.where(j == 0, qi, j - 1), sa_ref), sa_max, ma)
            sa_max, mb = phase((0, j, sa_ref, False), (1, j, sb_ref), sb_max, mb)
            return ma, mb, sa_max

        m0 = (jnp.full((1, cb), -jnp.inf, F32),) * len(subs)
        carry = lax.fori_loop(0, qi // 2, lambda i, c: body(2 * i + 1, body(2 * i, c)),
                              (m0, m0, sa_max))
        ma, mb, sa_max = lax.cond(qi % 2 == 1, lambda c: body(qi - 1, c), lambda c: c, carry)
        nxt = jnp.minimum(qi + 1, n_qtiles - 1)
        qqt_ref[0] = stacked_qt(nxt, 0)
        sb_max, _ = phase((1, qi, sb_ref, True),
                          (0, jnp.maximum(qi - 1, 0), sa_ref), sa_max, ma)
        qqt_ref[1] = stacked_qt(nxt, 1)
        finalize(qi, 0)
        phase(None, (1, qi, sb_ref), sb_max, mb)
        first = scores(0, nxt, sa_ref, True, 0)
        finalize(qi, 1)
        return (first,) + tuple(scores(0, nxt, sa_ref, True, sub) for sub in subs[1:])

    acc_ref[...] = jnp.zeros_like(acc_ref)
    for hh in heads:
        qqt_ref[hh] = stacked_qt(0, hh)
    lax.fori_loop(0, n_qtiles, q_tile,
                  tuple(scores(0, 0, sa_ref, True, sub) for sub in subs))


def _diff_attn(q, k, vt, bias, lamv, subln_g_col):
    b, s, _ = q.shape
    hw = 128 * HEADS_PER_STEP
    return pl.pallas_call(
        _attn_kernel,
        grid=(b, DA_HEADS // HEADS_PER_STEP),
        in_specs=[
            pl.BlockSpec((1, s, hw), lambda bi, hi: (bi, 0, hi)),
            pl.BlockSpec((1, s, hw), lambda bi, hi: (bi, 0, hi)),
            pl.BlockSpec((1, hw, s), lambda bi, hi: (bi, hi, 0)),
            _const_spec((TK, TQ)),
            _const_spec((4, DA_HEAD_DIM)),
            _const_spec((DA_V_DIM, 1)),
        ],
        out_specs=pl.BlockSpec((1, s, hw), lambda bi, hi: (bi, 0, hi)),
        out_shape=jax.ShapeDtypeStruct((b, s, V_WIDTH), BF16),
        scratch_shapes=[pltpu.VMEM((HEADS_PER_STEP, DA_V_DIM + BF16_SUBLANES, 2 * TQ), F32),
                        pltpu.VMEM((TK, 2 * TQ), F32),
                        pltpu.VMEM((TK, 2 * TQ), F32),
                        pltpu.VMEM((HEADS_PER_STEP, 128, 2 * TQ), BF16)],
        name="diff_attn",
        compiler_params=pltpu.CompilerParams(
            dimension_semantics=("arbitrary", "arbitrary"),
            vmem_limit_bytes=V7X_VMEM_LIMIT_BYTES),
    )(q, k, vt, bias, lamv, subln_g_col)


def _rope_tables(seq):
    half = DA_HEAD_DIM // 2
    inv_freq = 1.0 / (ROPE_THETA ** (jnp.arange(0, DA_HEAD_DIM, 2, dtype=F32) / DA_HEAD_DIM))
    ang = jnp.arange(seq, dtype=F32)[:, None] * inv_freq[None, :]
    cos, sin = jnp.cos(ang), jnp.sin(ang)
    cos128 = jnp.tile(cos, (1, 128 // half))
    sin128 = jnp.tile(jnp.concatenate([-sin, sin], axis=1), (1, 128 // DA_HEAD_DIM))
    return cos128, sin128


def kernel(x, norm_ffn1, ffn1_gate, ffn1_up, ffn1_down, norm_mix, w_in, b_gate, conv_w,
           w_conv_out, lambda_q1, lambda_k1, lambda_q2, lambda_k2, subln_g, w_attn_out, w_o,
           norm_ffn2, ffn2_gate, ffn2_up, ffn2_down, norm_final):
    bsz, seq, d = x.shape
    assert d == D_MODEL and norm_ffn1.shape[0] == 1
    assert seq % TQ == 0 and seq % TM_PROJ == 0
    assert (bsz * seq) % TM_FFN == 0 and (bsz * seq) % TM_MERGE_FFN == 0
    assert TQ == TK and TQ % CHUNK == 0
    assert HEADS_PER_STEP == 2 and DA_HEADS % HEADS_PER_STEP == 0
    t = bsz * seq
    l = 0
    bf = lambda a: a.astype(BF16)
    row = lambda a: a.reshape(1, -1)

    cos128, sin128 = _rope_tables(seq)
    scale = math.log2(math.e) / math.sqrt(DA_HEAD_DIM)
    pos_chunk = jnp.arange(TQ, dtype=jnp.int32) // CHUNK
    bias = jnp.where(pos_chunk[:, None] <= pos_chunk[None, :], 0.0, -jnp.inf).astype(F32)
    lamv = jnp.stack([lambda_q1[l], lambda_k1[l], lambda_q2[l], lambda_k2[l]]).astype(F32)

    h = _ffn(x.reshape(t, d), row(norm_ffn1[l]), bf(ffn1_gate[l]), bf(ffn1_up[l]),
             bf(ffn1_down[l]))
    gy, ga, q, k, vt = _in_proj(
        h.reshape(bsz, seq, d), row(norm_mix[l]), bf(w_in[l]), row(b_gate[l]), conv_w[l],
        bf(w_conv_out[l]), cos128 * scale, sin128 * scale, cos128, sin128)
    o = _diff_attn(q, k, vt, bias, lamv, subln_g[l].reshape(-1, 1))
    out = _merge_ffn(h, o.reshape(t, V_WIDTH), gy.reshape(t, d), ga.reshape(t, d),
                     bf(w_attn_out[l]), bf(w_o[l]), row(norm_ffn2[l]), bf(ffn2_gate[l]),
                     bf(ffn2_up[l]), bf(ffn2_down[l]), row(norm_final))
    return out.reshape(bsz, seq, d)
```

```python
import math

import jax
import jax.numpy as jnp
from jax import lax
from jax.experimental import pallas as pl
from jax.experimental.pallas import tpu as pltpu

D_MODEL = 1024
CHUNK = 64
CONV_WIDTH = 512
CONV_K = 3
DA_HEADS = 4
DA_HEAD_DIM = 64
DA_V_DIM = 128
QK_WIDTH = 512
V_WIDTH = 512
D_FF = 2816
ROPE_THETA = 10000.0
NORM_EPS = 1e-6
LAMBDA_INIT = 0.8 - 0.6 * math.exp(-0.3 * 0)

OFF_XC, OFF_BG, OFF_CG, OFF_Q, OFF_K, OFF_V, OFF_GC, OFF_GA = (
    0, 512, 1024, 1536, 2048, 2560, 3072, 4096)
IN_WIDTH = 5120

V7X_VMEM_LIMIT_BYTES = 56 * 1024 * 1024
BF16_SUBLANES = 16

TM_FFN = 1024
TM_MERGE_FFN = 512
TM_PROJ = 1024
TQ = 512
TK = 512
HEADS_PER_STEP = 2
ATTN_COL_BLOCK = 512
CONV_HALO = 8

BF16 = jnp.bfloat16
F32 = jnp.float32


def _const_spec(shape):
    return pl.BlockSpec(shape, lambda *_: (0,) * len(shape), pipeline_mode=pl.Buffered(1))


def _rms(x, g):
    return x * lax.rsqrt(jnp.mean(x * x, axis=-1, keepdims=True) + NORM_EPS) * g


def _sigmoid(x):
    return 0.5 * jnp.tanh(0.5 * x) + 0.5


def _mm(a, b):
    return jnp.dot(a, b, preferred_element_type=F32)


def _mm_nt(a, b):
    return lax.dot_general(a, b, (((1,), (1,)), ((), ())), preferred_element_type=F32)


def _ffn_kernel(x_ref, g_ref, wg_ref, wu_ref, wd_ref, o_ref):
    x = x_ref[...]
    xn = _rms(x, g_ref[...]).astype(BF16)
    gate = _mm(xn, wg_ref[...])
    up = _mm(xn, wu_ref[...])
    a = (gate * _sigmoid(gate) * up).astype(BF16)
    o_ref[...] = x + 0.5 * _mm(a, wd_ref[...])


def _merge_ffn_kernel(h_ref, o_ref, gy_ref, ga_ref, wao_ref, wo_ref, g_ref, wg_ref, wu_ref,
                      wd_ref, gf_ref, out_ref):
    half = h_ref.shape[0] // 2
    rows = [slice(0, half), slice(half, 2 * half)]

    def merge(r):
        y_attn = _mm(o_ref[r, :], wao_ref[...])
        mix = gy_ref[r, :].astype(F32) + ga_ref[r, :].astype(F32) * y_attn
        return h_ref[r, :] + _mm(mix.astype(BF16), wo_ref[...])

    def gate_up(xn):
        return _mm(xn, wg_ref[...]), _mm(xn, wu_ref[...])

    def down(h2, gate, up):
        a = (gate * _sigmoid(gate) * up).astype(BF16)
        return h2 + 0.5 * _mm(a, wd_ref[...])

    norm = lambda x: _rms(x, g_ref[...]).astype(BF16)
    h2a = merge(rows[0])
    h2b = merge(rows[1])
    gu_a = gate_up(norm(h2a))
    ya = down(h2a, *gu_a)
    gu_b = gate_up(norm(h2b))
    out_ref[rows[0], :] = _rms(ya, gf_ref[...])
    yb = down(h2b, *gu_b)
    out_ref[rows[1], :] = _rms(yb, gf_ref[...])


_FFN_WEIGHT_SPECS = lambda: [
    _const_spec((1, D_MODEL)),
    _const_spec((D_MODEL, D_FF)),
    _const_spec((D_MODEL, D_FF)),
    _const_spec((D_FF, D_MODEL)),
]


def _ffn(x2d, g, wg, wu, wd):
    t = x2d.shape[0]
    tok = lambda i: (i, 0)
    return pl.pallas_call(
        _ffn_kernel,
        grid=(t // TM_FFN,),
        in_specs=[pl.BlockSpec((TM_FFN, D_MODEL), tok)] + _FFN_WEIGHT_SPECS(),
        out_specs=pl.BlockSpec((TM_FFN, D_MODEL), tok),
        out_shape=jax.ShapeDtypeStruct((t, D_MODEL), F32),
        name="ffn",
        compiler_params=pltpu.CompilerParams(
            dimension_semantics=("arbitrary",), vmem_limit_bytes=V7X_VMEM_LIMIT_BYTES),
    )(x2d, g, wg, wu, wd)


def _merge_ffn(h2d, o2d, gy2d, ga2d, wao, wo, g, wg, wu, wd, gf):
    t = h2d.shape[0]
    tm = TM_MERGE_FFN
    tok = lambda i: (i, 0)
    return pl.pallas_call(
        _merge_ffn_kernel,
        grid=(t // tm,),
        in_specs=[
            pl.BlockSpec((tm, D_MODEL), tok),
            pl.BlockSpec((tm, V_WIDTH), tok),
            pl.BlockSpec((tm, D_MODEL), tok),
            pl.BlockSpec((tm, D_MODEL), tok),
            _const_spec((V_WIDTH, D_MODEL)),
            _const_spec((D_MODEL, D_MODEL)),
        ] + _FFN_WEIGHT_SPECS() + [_const_spec((1, D_MODEL))],
        out_specs=pl.BlockSpec((tm, D_MODEL), tok),
        out_shape=jax.ShapeDtypeStruct((t, D_MODEL), F32),
        name="merge_ffn_final",
        compiler_params=pltpu.CompilerParams(
            dimension_semantics=("arbitrary",), vmem_limit_bytes=V7X_VMEM_LIMIT_BYTES),
    )(h2d, o2d, gy2d, ga2d, wao, wo, g, wg, wu, wd, gf)


def _rope(x, cos, sin_signed, lo_half):
    outs = []
    for c in range(x.shape[1] // 128):
        xc = x[:, c * 128:(c + 1) * 128]
        rot = jnp.where(lo_half, pltpu.roll(xc, 96, 1), pltpu.roll(xc, 32, 1))
        outs.append(xc * cos + rot * sin_signed)
    return jnp.concatenate(outs, axis=1)


def _in_proj_kernel(h_ref, g_ref, w_ref, bgate_ref, convw_ref, wco_ref, cosq_ref, sinq_ref,
                    cosk_ref, sink_ref, gy_ref, ga_ref, q_ref, k_ref, vt_ref, zbuf_ref):
    tm = h_ref.shape[1]

    @pl.when(pl.program_id(1) == 0)
    def _():
        zbuf_ref[0:CONV_HALO, :] = jnp.zeros((CONV_HALO, CONV_WIDTH), F32)

    u = _rms(h_ref[0], g_ref[...]).astype(BF16)

    def proj(off, width):
        return _mm(u, w_ref[:, off:off + width])

    vt_ref[0] = proj(OFF_V, V_WIDTH).T.astype(BF16)
    lane = lax.broadcasted_iota(jnp.int32, (tm, 128), 1)
    lo_half = (lane % DA_HEAD_DIM) < (DA_HEAD_DIM // 2)
    q_ref[0] = _rope(proj(OFF_Q, QK_WIDTH), cosq_ref[...], sinq_ref[...], lo_half).astype(BF16)
    k_ref[0] = _rope(proj(OFF_K, QK_WIDTH), cosk_ref[...], sink_ref[...], lo_half).astype(BF16)

    ga_ref[0] = _sigmoid(proj(OFF_GA, D_MODEL) + bgate_ref[:, D_MODEL:]).astype(ga_ref.dtype)
    g_conv = _sigmoid(proj(OFF_GC, D_MODEL) + bgate_ref[:, 0:D_MODEL])

    z = proj(OFF_CG, CONV_WIDTH) * proj(OFF_XC, CONV_WIDTH)
    zbuf_ref[CONV_HALO:CONV_HALO + tm, :] = z
    cw = convw_ref[...]
    zc = (cw[0:1, :] * zbuf_ref[CONV_HALO - 2:CONV_HALO - 2 + tm, :]
          + cw[1:2, :] * zbuf_ref[CONV_HALO - 1:CONV_HALO - 1 + tm, :]
          + cw[2:3, :] * z)
    zbuf_ref[0:CONV_HALO, :] = z[tm - CONV_HALO:, :]
    ya = _mm((proj(OFF_BG, CONV_WIDTH) * zc).astype(BF16), wco_ref[...])
    gy_ref[0] = (g_conv * ya).astype(gy_ref.dtype)


def _in_proj(h3d, g, w_in, b_gate, conv_w, w_conv_out, cosq, sinq, cosk, sink):
    b, s, _ = h3d.shape
    tm = TM_PROJ
    tok = lambda bi, si: (bi, si, 0)
    tab = lambda bi, si: (si, 0)
    bsd = lambda width, dt: jax.ShapeDtypeStruct((b, s, width), dt)
    return pl.pallas_call(
        _in_proj_kernel,
        grid=(b, s // tm),
        in_specs=[
            pl.BlockSpec((1, tm, D_MODEL), tok),
            _const_spec((1, D_MODEL)),
            _const_spec((D_MODEL, IN_WIDTH)),
            _const_spec((1, 2 * D_MODEL)),
            _const_spec((CONV_K, CONV_WIDTH)),
            _const_spec((CONV_WIDTH, D_MODEL)),
            pl.BlockSpec((tm, 128), tab),
            pl.BlockSpec((tm, 128), tab),
            pl.BlockSpec((tm, 128), tab),
            pl.BlockSpec((tm, 128), tab),
        ],
        out_specs=[
            pl.BlockSpec((1, tm, D_MODEL), tok),
            pl.BlockSpec((1, tm, D_MODEL), tok),
            pl.BlockSpec((1, tm, QK_WIDTH), tok),
            pl.BlockSpec((1, tm, QK_WIDTH), tok),
            pl.BlockSpec((1, V_WIDTH, tm), lambda bi, si: (bi, 0, si)),
        ],
        out_shape=[bsd(D_MODEL, BF16), bsd(D_MODEL, BF16), bsd(QK_WIDTH, BF16),
                   bsd(QK_WIDTH, BF16), jax.ShapeDtypeStruct((b, V_WIDTH, s), BF16)],
        scratch_shapes=[pltpu.VMEM((CONV_HALO + tm, CONV_WIDTH), F32)],
        name="in_proj",
        compiler_params=pltpu.CompilerParams(
            dimension_semantics=("arbitrary", "arbitrary"),
            vmem_limit_bytes=V7X_VMEM_LIMIT_BYTES),
    )(h3d, g, w_in, b_gate, conv_w, w_conv_out, cosq, sinq, cosk, sink)


def _attn_kernel(q_ref, k_ref, vt_ref, bias_ref, lamv_ref, sg_ref, o_ref,
                 acc_ref, sa_ref, sb_ref, qqt_ref):
    tq = TQ
    n_qtiles = q_ref.shape[1] // tq
    heads = range(HEADS_PER_STEP)
    hcols = lambda hh: slice(hh * 128, (hh + 1) * 128)
    qrows = lambda qi: pl.ds(pl.multiple_of(qi * tq, tq), tq)

    feat = lax.broadcasted_iota(jnp.int32, (128, tq), 0)
    zero = jnp.zeros((128, tq), F32)
    ones = jnp.ones((BF16_SUBLANES, TK), BF16)
    lv = lamv_ref[...]
    lam = (jnp.exp(jnp.sum(lv[0:1] * lv[1:2], axis=-1, keepdims=True))
           - jnp.exp(jnp.sum(lv[2:3] * lv[3:4], axis=-1, keepdims=True)) + LAMBDA_INIT)
    out_gain = sg_ref[...] * (1.0 - LAMBDA_INIT)

    def stacked_qt(qi, hh):
        qt = q_ref[0, qrows(qi), hcols(hh)].astype(F32).T
        return jnp.concatenate([jnp.where(feat < DA_HEAD_DIM, qt, zero),
                                jnp.where(feat >= DA_HEAD_DIM, qt, zero)], axis=1).astype(BF16)

    cb = ATTN_COL_BLOCK
    subs = range(2 * tq // cb)

    def pieces(masked, sub):
        c0 = sub * cb
        if not masked:
            return [(slice(c0, c0 + cb), TK)]
        return [(slice(c0, c0 + cb // 2), TK // 2), (slice(c0 + cb // 2, c0 + cb), TK)]

    def scores(hh, tile, s_ref, masked, sub):
        start = pl.multiple_of(tile * TK, TK)
        maxes = []
        for cols, nk in pieces(masked, sub):
            s = _mm(k_ref[0, pl.ds(start, nk), hcols(hh)], qqt_ref[hh, :, cols])
            if masked:
                s = s + bias_ref[0:nk, cols.start % tq:(cols.start % tq) + cb // 2]
                if nk < TK:
                    s_ref[nk:TK, cols] = jnp.full((TK - nk, cb // 2), -jnp.inf, F32)
            s_ref[0:nk, cols] = s
            maxes.append(jnp.max(s, axis=0, keepdims=True))
        return jnp.concatenate(maxes, axis=1)

    def accumulate(hh, tile, s_ref, masked, s_max, m, sub):
        start = pl.multiple_of(tile * TK, TK)
        m_new = jnp.maximum(m, s_max)
        alpha = jnp.exp2(m - m_new)
        for cols, nk in pieces(masked, sub):
            blk = slice(cols.start - sub * cb, cols.stop - sub * cb)
            vt = jnp.concatenate([vt_ref[0, hcols(hh), pl.ds(start, nk)], ones[:, 0:nk]], axis=0)
            p = jnp.exp2((s_ref[0:nk, cols] - m_new[:, blk]).astype(BF16))
            acc_ref[hh, :, cols] = alpha[:, blk] * acc_ref[hh, :, cols] + _mm(vt, p)
        return m_new

    def phase(score_args, acc_args, s_maxes, ms):
        new_maxes, new_ms = [], []
        for sub in subs:
            if score_args is not None:
                new_maxes.append(scores(*score_args, sub))
            new_ms.append(accumulate(*acc_args, s_maxes[sub], ms[sub], sub))
        return tuple(new_maxes), tuple(new_ms)

    def finalize(qi, hh):
        o = acc_ref[hh, 0:DA_V_DIM, :] * (1.0 / acc_ref[hh, DA_V_DIM:DA_V_DIM + 1, :])
        o = o[:, :tq] - lam * o[:, tq:]
        o = o * lax.rsqrt(jnp.mean(o * o, axis=0, keepdims=True) + NORM_EPS) * out_gain
        o_ref[0, qrows(qi), hcols(hh)] = o.T.astype(o_ref.dtype)
        acc_ref[hh] = jnp.zeros(acc_ref.shape[1:], F32)

    def q_tile(qi, sa_max):
        def body(j, carry):
            ma, mb, sa_max = carry
            sb_max, ma = phase((1, j, sb_ref, False),
                               (0, jnp.where(j == 0, qi, j - 1), sa_ref, False), sa_max, ma)
            sa_max, mb = phase((0, j, sa_ref, False), (1, j, sb_ref, False), sb_max, mb)
            return ma, mb, sa_max

        m0 = (jnp.full((1, cb), -jnp.inf, F32),) * len(subs)
        carry = lax.fori_loop(0, qi // 2, lambda i, c: body(2 * i + 1, body(2 * i, c)),
                              (m0, m0, sa_max))
        ma, mb, sa_max = lax.cond(qi % 2 == 1, lambda c: body(qi - 1, c), lambda c: c, carry)
        nxt = jnp.minimum(qi + 1, n_qtiles - 1)
        qqt_ref[0] = stacked_qt(nxt, 0)
        sb_max, _ = phase((1, qi, sb_ref, True),
                          (0, jnp.maximum(qi - 1, 0), sa_ref, False), sa_max, ma)
        qqt_ref[1] = stacked_qt(nxt, 1)
        finalize(qi, 0)
        phase(None, (1, qi, sb_ref, True), sb_max, mb)
        first = scores(0, nxt, sa_ref, True, 0)
        finalize(qi, 1)
        return (first,) + tuple(scores(0, nxt, sa_ref, True, sub) for sub in subs[1:])

    acc_ref[...] = jnp.zeros_like(acc_ref)
    for hh in heads:
        qqt_ref[hh] = stacked_qt(0, hh)
    lax.fori_loop(0, n_qtiles, q_tile,
                  tuple(scores(0, 0, sa_ref, True, sub) for sub in subs))


def _diff_attn(q, k, vt, bias, lamv, subln_g_col):
    b, s, _ = q.shape
    hw = 128 * HEADS_PER_STEP
    return pl.pallas_call(
        _attn_kernel,
        grid=(b, DA_HEADS // HEADS_PER_STEP),
        in_specs=[
            pl.BlockSpec((1, s, hw), lambda bi, hi: (bi, 0, hi)),
            pl.BlockSpec((1, s, hw), lambda bi, hi: (bi, 0, hi)),
            pl.BlockSpec((1, hw, s), lambda bi, hi: (bi, hi, 0)),
            _const_spec((TK, TQ)),
            _const_spec((4, DA_HEAD_DIM)),
            _const_spec((DA_V_DIM, 1)),
        ],
        out_specs=pl.BlockSpec((1, s, hw), lambda bi, hi: (bi, 0, hi)),
        out_shape=jax.ShapeDtypeStruct((b, s, V_WIDTH), BF16),
        scratch_shapes=[pltpu.VMEM((HEADS_PER_STEP, DA_V_DIM + BF16_SUBLANES, 2 * TQ), F32),
                        pltpu.VMEM((TK, 2 * TQ), F32),
                        pltpu.VMEM((TK, 2 * TQ), F32),
                        pltpu.VMEM((HEADS_PER_STEP, 128, 2 * TQ), BF16)],
        name="diff_attn",
        compiler_params=pltpu.CompilerParams(
            dimension_semantics=("arbitrary", "arbitrary"),
            vmem_limit_bytes=V7X_VMEM_LIMIT_BYTES),
    )(q, k, vt, bias, lamv, subln_g_col)


def _rope_tables(seq):
    half = DA_HEAD_DIM // 2
    inv_freq = 1.0 / (ROPE_THETA ** (jnp.arange(0, DA_HEAD_DIM, 2, dtype=F32) / DA_HEAD_DIM))
    ang = jnp.arange(seq, dtype=F32)[:, None] * inv_freq[None, :]
    cos, sin = jnp.cos(ang), jnp.sin(ang)
    cos128 = jnp.tile(cos, (1, 128 // half))
    sin128 = jnp.tile(jnp.concatenate([-sin, sin], axis=1), (1, 128 // DA_HEAD_DIM))
    return cos128, sin128


def kernel(x, norm_ffn1, ffn1_gate, ffn1_up, ffn1_down, norm_mix, w_in, b_gate, conv_w,
           w_conv_out, lambda_q1, lambda_k1, lambda_q2, lambda_k2, subln_g, w_attn_out, w_o,
           norm_ffn2, ffn2_gate, ffn2_up, ffn2_down, norm_final):
    bsz, seq, d = x.shape
    assert d == D_MODEL and norm_ffn1.shape[0] == 1
    assert seq % TQ == 0 and seq % TM_PROJ == 0
    assert (bsz * seq) % TM_FFN == 0 and (bsz * seq) % TM_MERGE_FFN == 0
    assert TQ == TK and TQ % CHUNK == 0
    assert HEADS_PER_STEP == 2 and DA_HEADS % HEADS_PER_STEP == 0
    t = bsz * seq
    l = 0
    bf = lambda a: a.astype(BF16)
    row = lambda a: a.reshape(1, -1)

    cos128, sin128 = _rope_tables(seq)
    scale = math.log2(math.e) / math.sqrt(DA_HEAD_DIM)
    pos_chunk = jnp.arange(TQ, dtype=jnp.int32) // CHUNK
    bias = jnp.where(pos_chunk[:, None] <= pos_chunk[None, :], 0.0, -jnp.inf).astype(F32)
    lamv = jnp.stack([lambda_q1[l], lambda_k1[l], lambda_q2[l], lambda_k2[l]]).astype(F32)

    h = _ffn(x.reshape(t, d), row(norm_ffn1[l]), bf(ffn1_gate[l]), bf(ffn1_up[l]),
             bf(ffn1_down[l]))
    gy, ga, q, k, vt = _in_proj(
        h.reshape(bsz, seq, d), row(norm_mix[l]), bf(w_in[l]), row(b_gate[l]), conv_w[l],
        bf(w_conv_out[l]), cos128 * scale, sin128 * scale, cos128, sin128)
    o = _diff_attn(q, k, vt, bias, lamv, subln_g[l].reshape(-1, 1))
    out = _merge_ffn(h, o.reshape(t, V_WIDTH), gy.reshape(t, d), ga.reshape(t, d),
                     bf(w_attn_out[l]), bf(w_o[l]), row(norm_ffn2[l]), bf(ffn2_gate[l]),
                     bf(ffn2_up[l]), bf(ffn2_down[l]), row(norm_final))
    return out.reshape(bsz, seq, d)
```

```python
import math

import jax
import jax.numpy as jnp
from jax import lax
from jax.experimental import pallas as pl
from jax.experimental.pallas import tpu as pltpu

D_MODEL = 1024
CHUNK = 64
CONV_WIDTH = 512
CONV_K = 3
DA_HEADS = 4
DA_HEAD_DIM = 64
DA_V_DIM = 128
QK_WIDTH = 512
V_WIDTH = 512
D_FF = 2816
ROPE_THETA = 10000.0
NORM_EPS = 1e-6
LAMBDA_INIT = 0.8 - 0.6 * math.exp(-0.3 * 0)

OFF_XC, OFF_BG, OFF_CG, OFF_Q, OFF_K, OFF_V, OFF_GC, OFF_GA = (
    0, 512, 1024, 1536, 2048, 2560, 3072, 4096)
IN_WIDTH = 5120

V7X_VMEM_LIMIT_BYTES = 56 * 1024 * 1024
BF16_SUBLANES = 16

TM_FFN = 1024
TM_MERGE_FFN = 512
TM_PROJ = 1024
TQ = 512
TK = 512
HEADS_PER_STEP = 2
ATTN_COL_BLOCK = 512
CONV_HALO = 8

BF16 = jnp.bfloat16
F32 = jnp.float32


def _const_spec(shape):
    return pl.BlockSpec(shape, lambda *_: (0,) * len(shape), pipeline_mode=pl.Buffered(1))


def _rms(x, g):
    return x * lax.rsqrt(jnp.mean(x * x, axis=-1, keepdims=True) + NORM_EPS) * g


def _sigmoid(x):
    return 0.5 * jnp.tanh(0.5 * x) + 0.5


def _mm(a, b):
    return jnp.dot(a, b, preferred_element_type=F32)


def _mm_nt(a, b):
    return lax.dot_general(a, b, (((1,), (1,)), ((), ())), preferred_element_type=F32)


def _side_cast_specs(weights, n_steps, step_index):
    in_specs, out_specs, out_shapes = [], [], []
    for w in weights:
        rows, cols = w.shape
        assert rows % (n_steps * BF16_SUBLANES) == 0
        spec = pl.BlockSpec((rows // n_steps, cols), lambda *idx: (step_index(*idx), 0))
        in_specs.append(spec)
        out_specs.append(spec)
        out_shapes.append(jax.ShapeDtypeStruct(w.shape, BF16))
    return in_specs, out_specs, out_shapes


def _side_cast(src_refs, dst_refs):
    for src, dst in zip(src_refs, dst_refs):
        dst[...] = src[...].astype(BF16)


def _ffn_kernel(x_ref, g_ref, wg_ref, wu_ref, wd_ref, *rest):
    n_side = len(rest) // 2
    o_ref = rest[n_side]
    _side_cast(rest[:n_side], rest[n_side + 1:])
    x = x_ref[...]
    xn = _rms(x, g_ref[...]).astype(BF16)
    gate = _mm(xn, wg_ref[...])
    up = _mm(xn, wu_ref[...])
    a = (gate * _sigmoid(gate) * up).astype(BF16)
    o_ref[...] = x + 0.5 * _mm(a, wd_ref[...])


def _merge_ffn_kernel(h_ref, o_ref, gy_ref, ga_ref, wao_ref, wo_ref, g_ref, wg_ref, wu_ref,
                      wd_ref, gf_ref, out_ref):
    half = h_ref.shape[0] // 2
    rows = [slice(0, half), slice(half, 2 * half)]

    def merge(r):
        y_attn = _mm(o_ref[r, :], wao_ref[...])
        mix = gy_ref[r, :].astype(F32) + ga_ref[r, :].astype(F32) * y_attn
        return h_ref[r, :] + _mm(mix.astype(BF16), wo_ref[...])

    def gate_up(xn):
        return _mm(xn, wg_ref[...]), _mm(xn, wu_ref[...])

    def down(h2, gate, up):
        a = (gate * _sigmoid(gate) * up).astype(BF16)
        return h2 + 0.5 * _mm(a, wd_ref[...])

    norm = lambda x: _rms(x, g_ref[...]).astype(BF16)
    h2a = merge(rows[0])
    h2b = merge(rows[1])
    gu_a = gate_up(norm(h2a))
    ya = down(h2a, *gu_a)
    gu_b = gate_up(norm(h2b))
    out_ref[rows[0], :] = _rms(ya, gf_ref[...])
    yb = down(h2b, *gu_b)
    out_ref[rows[1], :] = _rms(yb, gf_ref[...])


_FFN_WEIGHT_SPECS = lambda: [
    _const_spec((1, D_MODEL)),
    _const_spec((D_MODEL, D_FF)),
    _const_spec((D_MODEL, D_FF)),
    _const_spec((D_FF, D_MODEL)),
]


def _ffn(x2d, g, wg, wu, wd, side_weights):
    t = x2d.shape[0]
    n_steps = t // TM_FFN
    tok = lambda i: (i, 0)
    side_in, side_out, side_shapes = _side_cast_specs(side_weights, n_steps, lambda i: i)
    outs = pl.pallas_call(
        _ffn_kernel,
        grid=(n_steps,),
        in_specs=[pl.BlockSpec((TM_FFN, D_MODEL), tok)] + _FFN_WEIGHT_SPECS() + side_in,
        out_specs=[pl.BlockSpec((TM_FFN, D_MODEL), tok)] + side_out,
        out_shape=[jax.ShapeDtypeStruct((t, D_MODEL), F32)] + side_shapes,
        name="ffn",
        compiler_params=pltpu.CompilerParams(
            dimension_semantics=("arbitrary",), vmem_limit_bytes=V7X_VMEM_LIMIT_BYTES),
    )(x2d, g, wg, wu, wd, *side_weights)
    return outs[0], outs[1:]


def _merge_ffn(h2d, o2d, gy2d, ga2d, wao, wo, g, wg, wu, wd, gf):
    t = h2d.shape[0]
    tm = TM_MERGE_FFN
    tok = lambda i: (i, 0)
    return pl.pallas_call(
        _merge_ffn_kernel,
        grid=(t // tm,),
        in_specs=[
            pl.BlockSpec((tm, D_MODEL), tok),
            pl.BlockSpec((tm, V_WIDTH), tok),
            pl.BlockSpec((tm, D_MODEL), tok),
            pl.BlockSpec((tm, D_MODEL), tok),
            _const_spec((V_WIDTH, D_MODEL)),
            _const_spec((D_MODEL, D_MODEL)),
        ] + _FFN_WEIGHT_SPECS() + [_const_spec((1, D_MODEL))],
        out_specs=pl.BlockSpec((tm, D_MODEL), tok),
        out_shape=jax.ShapeDtypeStruct((t, D_MODEL), F32),
        name="merge_ffn_final",
        compiler_params=pltpu.CompilerParams(
            dimension_semantics=("arbitrary",), vmem_limit_bytes=V7X_VMEM_LIMIT_BYTES),
    )(h2d, o2d, gy2d, ga2d, wao, wo, g, wg, wu, wd, gf)


def _rope(x, cos, sin_signed, lo_half):
    outs = []
    for c in range(x.shape[1] // 128):
        xc = x[:, c * 128:(c + 1) * 128]
        rot = jnp.where(lo_half, pltpu.roll(xc, 96, 1), pltpu.roll(xc, 32, 1))
        outs.append(xc * cos + rot * sin_signed)
    return jnp.concatenate(outs, axis=1)


def _in_proj_kernel(h_ref, g_ref, w_ref, bgate_ref, convw_ref, wco_ref, cosq_ref, sinq_ref,
                    cosk_ref, sink_ref, gy_ref, ga_ref, q_ref, k_ref, vt_ref, zbuf_ref):
    tm = h_ref.shape[1]

    @pl.when(pl.program_id(1) == 0)
    def _():
        zbuf_ref[0:CONV_HALO, :] = jnp.zeros((CONV_HALO, CONV_WIDTH), F32)

    u = _rms(h_ref[0], g_ref[...]).astype(BF16)

    def proj(off, width):
        return _mm(u, w_ref[:, off:off + width])

    vt_ref[0] = proj(OFF_V, V_WIDTH).T.astype(BF16)
    lane = lax.broadcasted_iota(jnp.int32, (tm, 128), 1)
    lo_half = (lane % DA_HEAD_DIM) < (DA_HEAD_DIM // 2)
    q_ref[0] = _rope(proj(OFF_Q, QK_WIDTH), cosq_ref[...], sinq_ref[...], lo_half).astype(BF16)
    k_ref[0] = _rope(proj(OFF_K, QK_WIDTH), cosk_ref[...], sink_ref[...], lo_half).astype(BF16)

    ga_ref[0] = _sigmoid(proj(OFF_GA, D_MODEL) + bgate_ref[:, D_MODEL:]).astype(ga_ref.dtype)
    g_conv = _sigmoid(proj(OFF_GC, D_MODEL) + bgate_ref[:, 0:D_MODEL])

    z = proj(OFF_CG, CONV_WIDTH) * proj(OFF_XC, CONV_WIDTH)
    zbuf_ref[CONV_HALO:CONV_HALO + tm, :] = z
    cw = convw_ref[...]
    zc = (cw[0:1, :] * zbuf_ref[CONV_HALO - 2:CONV_HALO - 2 + tm, :]
          + cw[1:2, :] * zbuf_ref[CONV_HALO - 1:CONV_HALO - 1 + tm, :]
          + cw[2:3, :] * z)
    zbuf_ref[0:CONV_HALO, :] = z[tm - CONV_HALO:, :]
    ya = _mm((proj(OFF_BG, CONV_WIDTH) * zc).astype(BF16), wco_ref[...])
    gy_ref[0] = (g_conv * ya).astype(gy_ref.dtype)


def _in_proj(h3d, g, w_in, b_gate, conv_w, w_conv_out, cosq, sinq, cosk, sink):
    b, s, _ = h3d.shape
    tm = TM_PROJ
    tok = lambda bi, si: (bi, si, 0)
    tab = lambda bi, si: (si, 0)
    bsd = lambda width, dt: jax.ShapeDtypeStruct((b, s, width), dt)
    return pl.pallas_call(
        _in_proj_kernel,
        grid=(b, s // tm),
        in_specs=[
            pl.BlockSpec((1, tm, D_MODEL), tok),
            _const_spec((1, D_MODEL)),
            _const_spec((D_MODEL, IN_WIDTH)),
            _const_spec((1, 2 * D_MODEL)),
            _const_spec((CONV_K, CONV_WIDTH)),
            _const_spec((CONV_WIDTH, D_MODEL)),
            pl.BlockSpec((tm, 128), tab),
            pl.BlockSpec((tm, 128), tab),
            pl.BlockSpec((tm, 128), tab),
            pl.BlockSpec((tm, 128), tab),
        ],
        out_specs=[
            pl.BlockSpec((1, tm, D_MODEL), tok),
            pl.BlockSpec((1, tm, D_MODEL), tok),
            pl.BlockSpec((1, tm, QK_WIDTH), tok),
            pl.BlockSpec((1, tm, QK_WIDTH), tok),
            pl.BlockSpec((1, V_WIDTH, tm), lambda bi, si: (bi, 0, si)),
        ],
        out_shape=[bsd(D_MODEL, BF16), bsd(D_MODEL, BF16), bsd(QK_WIDTH, BF16),
                   bsd(QK_WIDTH, BF16), jax.ShapeDtypeStruct((b, V_WIDTH, s), BF16)],
        scratch_shapes=[pltpu.VMEM((CONV_HALO + tm, CONV_WIDTH), F32)],
        name="in_proj",
        compiler_params=pltpu.CompilerParams(
            dimension_semantics=("arbitrary", "arbitrary"),
            vmem_limit_bytes=V7X_VMEM_LIMIT_BYTES),
    )(h3d, g, w_in, b_gate, conv_w, w_conv_out, cosq, sinq, cosk, sink)


def _attn_kernel(q_ref, k_ref, vt_ref, bias_ref, lamv_ref, sg_ref, *rest):
    n_side = (len(rest) - 5) // 2
    o_ref = rest[n_side]
    acc_ref, sa_ref, sb_ref, qqt_ref = rest[2 * n_side + 1:]
    _side_cast(rest[:n_side], rest[n_side + 1:2 * n_side + 1])
    tq = TQ
    n_qtiles = q_ref.shape[1] // tq
    heads = range(HEADS_PER_STEP)
    hcols = lambda hh: slice(hh * 128, (hh + 1) * 128)
    qrows = lambda qi: pl.ds(pl.multiple_of(qi * tq, tq), tq)

    feat = lax.broadcasted_iota(jnp.int32, (128, tq), 0)
    zero = jnp.zeros((128, tq), F32)
    ones = jnp.ones((BF16_SUBLANES, TK), BF16)
    lv = lamv_ref[...]
    lam = (jnp.exp(jnp.sum(lv[0:1] * lv[1:2], axis=-1, keepdims=True))
           - jnp.exp(jnp.sum(lv[2:3] * lv[3:4], axis=-1, keepdims=True)) + LAMBDA_INIT)
    out_gain = sg_ref[...] * (1.0 - LAMBDA_INIT)

    def stacked_qt(qi, hh):
        qt = q_ref[0, qrows(qi), hcols(hh)].astype(F32).T
        return jnp.concatenate([jnp.where(feat < DA_HEAD_DIM, qt, zero),
                                jnp.where(feat >= DA_HEAD_DIM, qt, zero)], axis=1).astype(BF16)

    cb = ATTN_COL_BLOCK
    subs = range(2 * tq // cb)

    def pieces(masked, sub):
        c0 = sub * cb
        if not masked:
            return [(slice(c0, c0 + cb), TK)]
        return [(slice(c0, c0 + cb // 2), TK // 2), (slice(c0 + cb // 2, c0 + cb), TK)]

    def scores(hh, tile, s_ref, masked, sub):
        start = pl.multiple_of(tile * TK, TK)
        maxes = []
        for cols, nk in pieces(masked, sub):
            s = _mm(k_ref[0, pl.ds(start, nk), hcols(hh)], qqt_ref[hh, :, cols])
            if masked:
                s = s + bias_ref[0:nk, cols.start % tq:(cols.start % tq) + cb // 2]
                if nk < TK:
                    s_ref[nk:TK, cols] = jnp.full((TK - nk, cb // 2), -jnp.inf, F32)
            s_ref[0:nk, cols] = s
            maxes.append(jnp.max(s, axis=0, keepdims=True))
        return jnp.concatenate(maxes, axis=1)

    def accumulate(hh, tile, s_ref, masked, s_max, m, sub):
        start = pl.multiple_of(tile * TK, TK)
        m_new = jnp.maximum(m, s_max)
        alpha = jnp.exp2(m - m_new)
        for cols, nk in pieces(masked, sub):
            blk = slice(cols.start - sub * cb, cols.stop - sub * cb)
            vt = jnp.concatenate([vt_ref[0, hcols(hh), pl.ds(start, nk)], ones[:, 0:nk]], axis=0)
            p = jnp.exp2((s_ref[0:nk, cols] - m_new[:, blk]).astype(BF16))
            acc_ref[hh, :, cols] = alpha[:, blk] * acc_ref[hh, :, cols] + _mm(vt, p)
        return m_new

    def phase(score_args, acc_args, s_maxes, ms):
        new_maxes, new_ms = [], []
        for sub in subs:
            if score_args is not None:
                new_maxes.append(scores(*score_args, sub))
            new_ms.append(accumulate(*acc_args, s_maxes[sub], ms[sub], sub))
        return tuple(new_maxes), tuple(new_ms)

    def finalize(qi, hh):
        o = acc_ref[hh, 0:DA_V_DIM, :] * (1.0 / acc_ref[hh, DA_V_DIM:DA_V_DIM + 1, :])
        o = o[:, :tq] - lam * o[:, tq:]
        o = o * lax.rsqrt(jnp.mean(o * o, axis=0, keepdims=True) + NORM_EPS) * out_gain
        o_ref[0, qrows(qi), hcols(hh)] = o.T.astype(o_ref.dtype)
        acc_ref[hh] = jnp.zeros(acc_ref.shape[1:], F32)

    def q_tile(qi, sa_max):
        def body(j, carry):
            ma, mb, sa_max = carry
            sb_max, ma = phase((1, j, sb_ref, False),
                               (0, jnp.where(j == 0, qi, j - 1), sa_ref, False), sa_max, ma)
            sa_max, mb = phase((0, j, sa_ref, False), (1, j, sb_ref, False), sb_max, mb)
            return ma, mb, sa_max

        m0 = (jnp.full((1, cb), -jnp.inf, F32),) * len(subs)
        carry = lax.fori_loop(0, qi // 2, lambda i, c: body(2 * i + 1, body(2 * i, c)),
                              (m0, m0, sa_max))
        ma, mb, sa_max = lax.cond(qi % 2 == 1, lambda c: body(qi - 1, c), lambda c: c, carry)
        nxt = jnp.minimum(qi + 1, n_qtiles - 1)
        qqt_ref[0] = stacked_qt(nxt, 0)
        sb_max, _ = phase((1, qi, sb_ref, True),
                          (0, jnp.maximum(qi - 1, 0), sa_ref, False), sa_max, ma)
        qqt_ref[1] = stacked_qt(nxt, 1)
        finalize(qi, 0)
        phase(None, (1, qi, sb_ref, True), sb_max, mb)
        first = scores(0, nxt, sa_ref, True, 0)
        finalize(qi, 1)
        return (first,) + tuple(scores(0, nxt, sa_ref, True, sub) for sub in subs[1:])

    acc_ref[...] = jnp.zeros_like(acc_ref)
    for hh in heads:
        qqt_ref[hh] = stacked_qt(0, hh)
    lax.fori_loop(0, n_qtiles, q_tile,
                  tuple(scores(0, 0, sa_ref, True, sub) for sub in subs))


def _diff_attn(q, k, vt, bias, lamv, subln_g_col, side_weights):
    b, s, _ = q.shape
    hw = 128 * HEADS_PER_STEP
    n_pairs = DA_HEADS // HEADS_PER_STEP
    side_in, side_out, side_shapes = _side_cast_specs(
        side_weights, b * n_pairs, lambda bi, hi: bi * n_pairs + hi)
    outs = pl.pallas_call(
        _attn_kernel,
        grid=(b, n_pairs),
        in_specs=[
            pl.BlockSpec((1, s, hw), lambda bi, hi: (bi, 0, hi)),
            pl.BlockSpec((1, s, hw), lambda bi, hi: (bi, 0, hi)),
            pl.BlockSpec((1, hw, s), lambda bi, hi: (bi, hi, 0)),
            _const_spec((TK, TQ)),
            _const_spec((4, DA_HEAD_DIM)),
            _const_spec((DA_V_DIM, 1)),
        ] + side_in,
        out_specs=[pl.BlockSpec((1, s, hw), lambda bi, hi: (bi, 0, hi))] + side_out,
        out_shape=[jax.ShapeDtypeStruct((b, s, V_WIDTH), BF16)] + side_shapes,
        scratch_shapes=[pltpu.VMEM((HEADS_PER_STEP, DA_V_DIM + BF16_SUBLANES, 2 * TQ), F32),
                        pltpu.VMEM((TK, 2 * TQ), F32),
                        pltpu.VMEM((TK, 2 * TQ), F32),
                        pltpu.VMEM((HEADS_PER_STEP, 128, 2 * TQ), BF16)],
        name="diff_attn",
        compiler_params=pltpu.CompilerParams(
            dimension_semantics=("arbitrary", "arbitrary"),
            vmem_limit_bytes=V7X_VMEM_LIMIT_BYTES),
    )(q, k, vt, bias, lamv, subln_g_col, *side_weights)
    return outs[0], outs[1:]


def _rope_tables(seq):
    half = DA_HEAD_DIM // 2
    inv_freq = 1.0 / (ROPE_THETA ** (jnp.arange(0, DA_HEAD_DIM, 2, dtype=F32) / DA_HEAD_DIM))
    ang = jnp.arange(seq, dtype=F32)[:, None] * inv_freq[None, :]
    cos, sin = jnp.cos(ang), jnp.sin(ang)
    cos128 = jnp.tile(cos, (1, 128 // half))
    sin128 = jnp.tile(jnp.concatenate([-sin, sin], axis=1), (1, 128 // DA_HEAD_DIM))
    return cos128, sin128


def kernel(x, norm_ffn1, ffn1_gate, ffn1_up, ffn1_down, norm_mix, w_in, b_gate, conv_w,
           w_conv_out, lambda_q1, lambda_k1, lambda_q2, lambda_k2, subln_g, w_attn_out, w_o,
           norm_ffn2, ffn2_gate, ffn2_up, ffn2_down, norm_final):
    bsz, seq, d = x.shape
    assert d == D_MODEL and norm_ffn1.shape[0] == 1
    assert seq % TQ == 0 and seq % TM_PROJ == 0
    assert (bsz * seq) % TM_FFN == 0 and (bsz * seq) % TM_MERGE_FFN == 0
    assert TQ == TK and TQ % CHUNK == 0
    assert HEADS_PER_STEP == 2 and DA_HEADS % HEADS_PER_STEP == 0
    t = bsz * seq
    l = 0
    bf = lambda a: a.astype(BF16)
    row = lambda a: a.reshape(1, -1)

    cos128, sin128 = _rope_tables(seq)
    scale = math.log2(math.e) / math.sqrt(DA_HEAD_DIM)
    pos_chunk = jnp.arange(TQ, dtype=jnp.int32) // CHUNK
    bias = jnp.where(pos_chunk[:, None] <= pos_chunk[None, :], 0.0, -jnp.inf).astype(F32)
    lamv = jnp.stack([lambda_q1[l], lambda_k1[l], lambda_q2[l], lambda_k2[l]]).astype(F32)

    h, (w_in_bf, w_conv_out_bf) = _ffn(
        x.reshape(t, d), row(norm_ffn1[l]), bf(ffn1_gate[l]), bf(ffn1_up[l]), bf(ffn1_down[l]),
        side_weights=[w_in[l], w_conv_out[l]])
    gy, ga, q, k, vt = _in_proj(
        h.reshape(bsz, seq, d), row(norm_mix[l]), w_in_bf, row(b_gate[l]), conv_w[l],
        w_conv_out_bf, cos128 * scale, sin128 * scale, cos128, sin128)
    o, (w_attn_out_bf, w_o_bf, gate2_bf, up2_bf, down2_bf) = _diff_attn(
        q, k, vt, bias, lamv, subln_g[l].reshape(-1, 1),
        side_weights=[w_attn_out[l], w_o[l], ffn2_gate[l], ffn2_up[l], ffn2_down[l]])
    out = _merge_ffn(h, o.reshape(t, V_WIDTH), gy.reshape(t, d), ga.reshape(t, d),
                     w_attn_out_bf, w_o_bf, row(norm_ffn2[l]), gate2_bf, up2_bf, down2_bf,
                     row(norm_final))
    return out.reshape(bsz, seq, d)
```

```python
import math

import jax
import jax.numpy as jnp
from jax import lax
from jax.experimental import pallas as pl
from jax.experimental.pallas import tpu as pltpu

D_MODEL = 1024
CHUNK = 64
CONV_WIDTH = 512
CONV_K = 3
DA_HEADS = 4
DA_HEAD_DIM = 64
DA_V_DIM = 128
QK_WIDTH = 512
V_WIDTH = 512
D_FF = 2816
ROPE_THETA = 10000.0
NORM_EPS = 1e-6
LAMBDA_INIT = 0.8 - 0.6 * math.exp(-0.3 * 0)

OFF_XC, OFF_BG, OFF_CG, OFF_Q, OFF_K, OFF_V, OFF_GC, OFF_GA = (
    0, 512, 1024, 1536, 2048, 2560, 3072, 4096)
IN_WIDTH = 5120

V7X_VMEM_LIMIT_BYTES = 56 * 1024 * 1024
BF16_SUBLANES = 16

TM_FFN = 1024
TM_MERGE_FFN = 512
TM_PROJ = 1024
TQ = 512
TK = 512
HEADS_PER_STEP = 2
ATTN_COL_BLOCK = 512
CONV_HALO = 8

BF16 = jnp.bfloat16
F32 = jnp.float32


def _const_spec(shape):
    return pl.BlockSpec(shape, lambda *_: (0,) * len(shape), pipeline_mode=pl.Buffered(1))


def _rms(x, g):
    return x * lax.rsqrt(jnp.mean(x * x, axis=-1, keepdims=True) + NORM_EPS) * g


def _sigmoid(x):
    return 0.5 * jnp.tanh(0.5 * x) + 0.5


def _mm(a, b):
    return jnp.dot(a, b, preferred_element_type=F32)


def _mm_nt(a, b):
    return lax.dot_general(a, b, (((1,), (1,)), ((), ())), preferred_element_type=F32)


def _side_cast_specs(weights, n_steps, step_index):
    in_specs, out_specs, out_shapes = [], [], []
    for w in weights:
        rows, cols = w.shape
        assert rows % (n_steps * BF16_SUBLANES) == 0
        spec = pl.BlockSpec((rows // n_steps, cols), lambda *idx: (step_index(*idx), 0))
        in_specs.append(spec)
        out_specs.append(spec)
        out_shapes.append(jax.ShapeDtypeStruct(w.shape, BF16))
    return in_specs, out_specs, out_shapes


def _side_cast(src_refs, dst_refs):
    for src, dst in zip(src_refs, dst_refs):
        dst[...] = src[...].astype(BF16)


def _ffn_kernel(x_ref, g_ref, wg_ref, wu_ref, wd_ref, *rest):
    n_side = len(rest) // 2
    o_ref = rest[n_side]
    _side_cast(rest[:n_side], rest[n_side + 1:])
    x = x_ref[...]
    xn = _rms(x, g_ref[...]).astype(BF16)
    gate = _mm(xn, wg_ref[...])
    up = _mm(xn, wu_ref[...])
    a = (gate * _sigmoid(gate) * up).astype(BF16)
    o_ref[...] = x + 0.5 * _mm(a, wd_ref[...])


def _merge_ffn_kernel(h_ref, o_ref, gy_ref, ga_ref, wao_ref, wo_ref, g_ref, wg_ref, wu_ref,
                      wd_ref, gf_ref, out_ref):
    half = h_ref.shape[0] // 2
    rows = [slice(0, half), slice(half, 2 * half)]

    def merge(r):
        y_attn = _mm(o_ref[r, :], wao_ref[...])
        mix = gy_ref[r, :].astype(F32) + ga_ref[r, :].astype(F32) * y_attn
        return h_ref[r, :] + _mm(mix.astype(BF16), wo_ref[...])

    def gate_up(xn):
        return _mm(xn, wg_ref[...]), _mm(xn, wu_ref[...])

    def down(h2, gate, up):
        a = (gate * _sigmoid(gate) * up).astype(BF16)
        return h2 + 0.5 * _mm(a, wd_ref[...])

    norm = lambda x: _rms(x, g_ref[...]).astype(BF16)
    h2a = merge(rows[0])
    h2b = merge(rows[1])
    gu_a = gate_up(norm(h2a))
    ya = down(h2a, *gu_a)
    gu_b = gate_up(norm(h2b))
    out_ref[rows[0], :] = _rms(ya, gf_ref[...])
    yb = down(h2b, *gu_b)
    out_ref[rows[1], :] = _rms(yb, gf_ref[...])


_FFN_WEIGHT_SPECS = lambda: [
    _const_spec((1, D_MODEL)),
    _const_spec((D_MODEL, D_FF)),
    _const_spec((D_MODEL, D_FF)),
    _const_spec((D_FF, D_MODEL)),
]


def _ffn(x2d, g, wg, wu, wd, side_weights):
    t = x2d.shape[0]
    n_steps = t // TM_FFN
    tok = lambda i: (i, 0)
    side_in, side_out, side_shapes = _side_cast_specs(side_weights, n_steps, lambda i: i)
    outs = pl.pallas_call(
        _ffn_kernel,
        grid=(n_steps,),
        in_specs=[pl.BlockSpec((TM_FFN, D_MODEL), tok)] + _FFN_WEIGHT_SPECS() + side_in,
        out_specs=[pl.BlockSpec((TM_FFN, D_MODEL), tok)] + side_out,
        out_shape=[jax.ShapeDtypeStruct((t, D_MODEL), F32)] + side_shapes,
        name="ffn",
        compiler_params=pltpu.CompilerParams(
            dimension_semantics=("arbitrary",), vmem_limit_bytes=V7X_VMEM_LIMIT_BYTES),
    )(x2d, g, wg, wu, wd, *side_weights)
    return outs[0], outs[1:]


def _merge_ffn(h2d, o2d, gy2d, ga2d, wao, wo, g, wg, wu, wd, gf):
    t = h2d.shape[0]
    tm = TM_MERGE_FFN
    tok = lambda i: (i, 0)
    return pl.pallas_call(
        _merge_ffn_kernel,
        grid=(t // tm,),
        in_specs=[
            pl.BlockSpec((tm, D_MODEL), tok),
            pl.BlockSpec((tm, V_WIDTH), tok),
            pl.BlockSpec((tm, D_MODEL), tok),
            pl.BlockSpec((tm, D_MODEL), tok),
            _const_spec((V_WIDTH, D_MODEL)),
            _const_spec((D_MODEL, D_MODEL)),
        ] + _FFN_WEIGHT_SPECS() + [_const_spec((1, D_MODEL))],
        out_specs=pl.BlockSpec((tm, D_MODEL), tok),
        out_shape=jax.ShapeDtypeStruct((t, D_MODEL), F32),
        name="merge_ffn_final",
        compiler_params=pltpu.CompilerParams(
            dimension_semantics=("arbitrary",), vmem_limit_bytes=V7X_VMEM_LIMIT_BYTES),
    )(h2d, o2d, gy2d, ga2d, wao, wo, g, wg, wu, wd, gf)


def _rope(x, cos, sin_signed, lo_half):
    outs = []
    for c in range(x.shape[1] // 128):
        xc = x[:, c * 128:(c + 1) * 128]
        rot = jnp.where(lo_half, pltpu.roll(xc, 96, 1), pltpu.roll(xc, 32, 1))
        outs.append(xc * cos + rot * sin_signed)
    return jnp.concatenate(outs, axis=1)


def _in_proj_kernel(h_ref, g_ref, w_ref, bgate_ref, convw_ref, wco_ref, cosq_ref, sinq_ref,
                    cosk_ref, sink_ref, gy_ref, ga_ref, q_ref, k_ref, vt_ref, zbuf_ref):
    tm = h_ref.shape[1]

    @pl.when(pl.program_id(1) == 0)
    def _():
        zbuf_ref[0:CONV_HALO, :] = jnp.zeros((CONV_HALO, CONV_WIDTH), F32)

    u = _rms(h_ref[0], g_ref[...]).astype(BF16)

    def proj(off, width):
        return _mm(u, w_ref[:, off:off + width])

    vt_ref[0] = proj(OFF_V, V_WIDTH).T.astype(BF16)
    lane = lax.broadcasted_iota(jnp.int32, (tm, 128), 1)
    lo_half = (lane % DA_HEAD_DIM) < (DA_HEAD_DIM // 2)
    q_ref[0] = _rope(proj(OFF_Q, QK_WIDTH), cosq_ref[...], sinq_ref[...], lo_half).astype(BF16)
    k_ref[0] = _rope(proj(OFF_K, QK_WIDTH), cosk_ref[...], sink_ref[...], lo_half).astype(BF16)

    ga_ref[0] = _sigmoid(proj(OFF_GA, D_MODEL) + bgate_ref[:, D_MODEL:]).astype(ga_ref.dtype)
    g_conv = _sigmoid(proj(OFF_GC, D_MODEL) + bgate_ref[:, 0:D_MODEL])

    z = proj(OFF_CG, CONV_WIDTH) * proj(OFF_XC, CONV_WIDTH)
    zbuf_ref[CONV_HALO:CONV_HALO + tm, :] = z
    cw = convw_ref[...]
    zc = (cw[0:1, :] * zbuf_ref[CONV_HALO - 2:CONV_HALO - 2 + tm, :]
          + cw[1:2, :] * zbuf_ref[CONV_HALO - 1:CONV_HALO - 1 + tm, :]
          + cw[2:3, :] * z)
    zbuf_ref[0:CONV_HALO, :] = z[tm - CONV_HALO:, :]
    ya = _mm((proj(OFF_BG, CONV_WIDTH) * zc).astype(BF16), wco_ref[...])
    gy_ref[0] = (g_conv * ya).astype(gy_ref.dtype)


def _in_proj(h3d, g, w_in, b_gate, conv_w, w_conv_out, cosq, sinq, cosk, sink):
    b, s, _ = h3d.shape
    tm = TM_PROJ
    tok = lambda bi, si: (bi, si, 0)
    tab = lambda bi, si: (si, 0)
    bsd = lambda width, dt: jax.ShapeDtypeStruct((b, s, width), dt)
    return pl.pallas_call(
        _in_proj_kernel,
        grid=(b, s // tm),
        in_specs=[
            pl.BlockSpec((1, tm, D_MODEL), tok),
            _const_spec((1, D_MODEL)),
            _const_spec((D_MODEL, IN_WIDTH)),
            _const_spec((1, 2 * D_MODEL)),
            _const_spec((CONV_K, CONV_WIDTH)),
            _const_spec((CONV_WIDTH, D_MODEL)),
            pl.BlockSpec((tm, 128), tab),
            pl.BlockSpec((tm, 128), tab),
            pl.BlockSpec((tm, 128), tab),
            pl.BlockSpec((tm, 128), tab),
        ],
        out_specs=[
            pl.BlockSpec((1, tm, D_MODEL), tok),
            pl.BlockSpec((1, tm, D_MODEL), tok),
            pl.BlockSpec((1, tm, QK_WIDTH), tok),
            pl.BlockSpec((1, tm, QK_WIDTH), tok),
            pl.BlockSpec((1, V_WIDTH, tm), lambda bi, si: (bi, 0, si)),
        ],
        out_shape=[bsd(D_MODEL, BF16), bsd(D_MODEL, BF16), bsd(QK_WIDTH, BF16),
                   bsd(QK_WIDTH, BF16), jax.ShapeDtypeStruct((b, V_WIDTH, s), BF16)],
        scratch_shapes=[pltpu.VMEM((CONV_HALO + tm, CONV_WIDTH), F32)],
        name="in_proj",
        compiler_params=pltpu.CompilerParams(
            dimension_semantics=("arbitrary", "arbitrary"),
            vmem_limit_bytes=V7X_VMEM_LIMIT_BYTES),
    )(h3d, g, w_in, b_gate, conv_w, w_conv_out, cosq, sinq, cosk, sink)


def _attn_kernel(q_ref, k_ref, vt_ref, bias_ref, lamv_ref, sg_ref, *rest):
    n_side = (len(rest) - 5) // 2
    o_ref = rest[n_side]
    acc_ref, sa_ref, sb_ref, qqt_ref = rest[2 * n_side + 1:]
    _side_cast(rest[:n_side], rest[n_side + 1:2 * n_side + 1])
    tq = TQ
    n_qtiles = q_ref.shape[1] // tq
    heads = range(HEADS_PER_STEP)
    hcols = lambda hh: slice(hh * 128, (hh + 1) * 128)
    qrows = lambda qi: pl.ds(pl.multiple_of(qi * tq, tq), tq)

    feat = lax.broadcasted_iota(jnp.int32, (128, tq), 0)
    zero = jnp.zeros((128, tq), F32)
    ones = jnp.ones((BF16_SUBLANES, TK), BF16)
    lv = lamv_ref[...]
    lam = (jnp.exp(jnp.sum(lv[0:1] * lv[1:2], axis=-1, keepdims=True))
           - jnp.exp(jnp.sum(lv[2:3] * lv[3:4], axis=-1, keepdims=True)) + LAMBDA_INIT)
    out_gain = sg_ref[...] * (1.0 - LAMBDA_INIT)

    def stacked_qt(qi, hh):
        qt = q_ref[0, qrows(qi), hcols(hh)].astype(F32).T
        return jnp.concatenate([jnp.where(feat < DA_HEAD_DIM, qt, zero),
                                jnp.where(feat >= DA_HEAD_DIM, qt, zero)], axis=1).astype(BF16)

    cb = ATTN_COL_BLOCK
    subs = range(2 * tq // cb)

    def pieces(masked, sub):
        c0 = sub * cb
        if not masked:
            return [(slice(c0, c0 + cb), TK)]
        return [(slice(c0, c0 + cb // 2), TK // 2), (slice(c0 + cb // 2, c0 + cb), TK)]

    def scores(hh, tile, s_ref, masked, sub):
        start = pl.multiple_of(tile * TK, TK)
        maxes = []
        for cols, nk in pieces(masked, sub):
            s = _mm(k_ref[0, pl.ds(start, nk), hcols(hh)], qqt_ref[hh, :, cols])
            if masked:
                s = s + bias_ref[0:nk, cols.start % tq:(cols.start % tq) + cb // 2]
                if nk < TK:
                    s_ref[nk:TK, cols] = jnp.full((TK - nk, cb // 2), -jnp.inf, F32)
            s_ref[0:nk, cols] = s
            maxes.append(jnp.max(s, axis=0, keepdims=True))
        return jnp.concatenate(maxes, axis=1)

    def accumulate(hh, tile, s_ref, masked, s_max, m, sub):
        start = pl.multiple_of(tile * TK, TK)
        m_new = jnp.maximum(m, s_max)
        alpha = jnp.exp2(m - m_new)
        for cols, nk in pieces(masked, sub):
            blk = slice(cols.start - sub * cb, cols.stop - sub * cb)
            vt = jnp.concatenate([vt_ref[0, hcols(hh), pl.ds(start, nk)], ones[:, 0:nk]], axis=0)
            p = jnp.exp2(s_ref[0:nk, cols] - m_new[:, blk]).astype(BF16)
            acc_ref[hh, :, cols] = alpha[:, blk] * acc_ref[hh, :, cols] + _mm(vt, p)
        return m_new

    def phase(score_args, acc_args, s_maxes, ms):
        new_maxes, new_ms = [], []
        for sub in subs:
            if score_args is not None:
                new_maxes.append(scores(*score_args, sub))
            new_ms.append(accumulate(*acc_args, s_maxes[sub], ms[sub], sub))
        return tuple(new_maxes), tuple(new_ms)

    def finalize(qi, hh):
        o = acc_ref[hh, 0:DA_V_DIM, :] * (1.0 / acc_ref[hh, DA_V_DIM:DA_V_DIM + 1, :])
        o = o[:, :tq] - lam * o[:, tq:]
        o = o * lax.rsqrt(jnp.mean(o * o, axis=0, keepdims=True) + NORM_EPS) * out_gain
        o_ref[0, qrows(qi), hcols(hh)] = o.T.astype(o_ref.dtype)
        acc_ref[hh] = jnp.zeros(acc_ref.shape[1:], F32)

    def q_tile(qi, sa_max):
        def body(j, carry):
            ma, mb, sa_max = carry
            sb_max, ma = phase((1, j, sb_ref, False),
                               (0, jnp.where(j == 0, qi, j - 1), sa_ref, False), sa_max, ma)
            sa_max, mb = phase((0, j, sa_ref, False), (1, j, sb_ref, False), sb_max, mb)
            return ma, mb, sa_max

        m0 = (jnp.full((1, cb), -jnp.inf, F32),) * len(subs)
        carry = lax.fori_loop(0, qi // 2, lambda i, c: body(2 * i + 1, body(2 * i, c)),
                              (m0, m0, sa_max))
        ma, mb, sa_max = lax.cond(qi % 2 == 1, lambda c: body(qi - 1, c), lambda c: c, carry)
        nxt = jnp.minimum(qi + 1, n_qtiles - 1)
        qqt_ref[0] = stacked_qt(nxt, 0)
        sb_max, _ = phase((1, qi, sb_ref, True),
                          (0, jnp.maximum(qi - 1, 0), sa_ref, False), sa_max, ma)
        qqt_ref[1] = stacked_qt(nxt, 1)
        finalize(qi, 0)
        phase(None, (1, qi, sb_ref, True), sb_max, mb)
        first = scores(0, nxt, sa_ref, True, 0)
        finalize(qi, 1)
        return (first,) + tuple(scores(0, nxt, sa_ref, True, sub) for sub in subs[1:])

    acc_ref[...] = jnp.zeros_like(acc_ref)
    for hh in heads:
        qqt_ref[hh] = stacked_qt(0, hh)
    lax.fori_loop(0, n_qtiles, q_tile,
                  tuple(scores(0, 0, sa_ref, True, sub) for sub in subs))


def _diff_attn(q, k, vt, bias, lamv, subln_g_col, side_weights):
    b, s, _ = q.shape
    hw = 128 * HEADS_PER_STEP
    n_pairs = DA_HEADS // HEADS_PER_STEP
    side_in, side_out, side_shapes = _side_cast_specs(
        side_weights, b * n_pairs, lambda bi, hi: bi * n_pairs + hi)
    outs = pl.pallas_call(
        _attn_kernel,
        grid=(b, n_pairs),
        in_specs=[
            pl.BlockSpec((1, s, hw), lambda bi, hi: (bi, 0, hi)),
            pl.BlockSpec((1, s, hw), lambda bi, hi: (bi, 0, hi)),
            pl.BlockSpec((1, hw, s), lambda bi, hi: (bi, hi, 0)),
            _const_spec((TK, TQ)),
            _const_spec((4, DA_HEAD_DIM)),
            _const_spec((DA_V_DIM, 1)),
        ] + side_in,
        out_specs=[pl.BlockSpec((1, s, hw), lambda bi, hi: (bi, 0, hi))] + side_out,
        out_shape=[jax.ShapeDtypeStruct((b, s, V_WIDTH), BF16)] + side_shapes,
        scratch_shapes=[pltpu.VMEM((HEADS_PER_STEP, DA_V_DIM + BF16_SUBLANES, 2 * TQ), F32),
                        pltpu.VMEM((TK, 2 * TQ), F32),
                        pltpu.VMEM((TK, 2 * TQ), F32),
                        pltpu.VMEM((HEADS_PER_STEP, 128, 2 * TQ), BF16)],
        name="diff_attn",
        compiler_params=pltpu.CompilerParams(
            dimension_semantics=("arbitrary", "arbitrary"),
            vmem_limit_bytes=V7X_VMEM_LIMIT_BYTES),
    )(q, k, vt, bias, lamv, subln_g_col, *side_weights)
    return outs[0], outs[1:]


def _rope_tables(seq):
    half = DA_HEAD_DIM // 2
    inv_freq = 1.0 / (ROPE_THETA ** (jnp.arange(0, DA_HEAD_DIM, 2, dtype=F32) / DA_HEAD_DIM))
    ang = jnp.arange(seq, dtype=F32)[:, None] * inv_freq[None, :]
    cos, sin = jnp.cos(ang), jnp.sin(ang)
    cos128 = jnp.tile(cos, (1, 128 // half))
    sin128 = jnp.tile(jnp.concatenate([-sin, sin], axis=1), (1, 128 // DA_HEAD_DIM))
    return cos128, sin128


def kernel(x, norm_ffn1, ffn1_gate, ffn1_up, ffn1_down, norm_mix, w_in, b_gate, conv_w,
           w_conv_out, lambda_q1, lambda_k1, lambda_q2, lambda_k2, subln_g, w_attn_out, w_o,
           norm_ffn2, ffn2_gate, ffn2_up, ffn2_down, norm_final):
    bsz, seq, d = x.shape
    assert d == D_MODEL and norm_ffn1.shape[0] == 1
    assert seq % TQ == 0 and seq % TM_PROJ == 0
    assert (bsz * seq) % TM_FFN == 0 and (bsz * seq) % TM_MERGE_FFN == 0
    assert TQ == TK and TQ % CHUNK == 0
    assert HEADS_PER_STEP == 2 and DA_HEADS % HEADS_PER_STEP == 0
    t = bsz * seq
    l = 0
    bf = lambda a: a.astype(BF16)
    row = lambda a: a.reshape(1, -1)

    cos128, sin128 = _rope_tables(seq)
    scale = math.log2(math.e) / math.sqrt(DA_HEAD_DIM)
    pos_chunk = jnp.arange(TQ, dtype=jnp.int32) // CHUNK
    bias = jnp.where(pos_chunk[:, None] <= pos_chunk[None, :], 0.0, -jnp.inf).astype(F32)
    lamv = jnp.stack([lambda_q1[l], lambda_k1[l], lambda_q2[l], lambda_k2[l]]).astype(F32)

    h, (w_in_bf, w_conv_out_bf) = _ffn(
        x.reshape(t, d), row(norm_ffn1[l]), bf(ffn1_gate[l]), bf(ffn1_up[l]), bf(ffn1_down[l]),
        side_weights=[w_in[l], w_conv_out[l]])
    gy, ga, q, k, vt = _in_proj(
        h.reshape(bsz, seq, d), row(norm_mix[l]), w_in_bf, row(b_gate[l]), conv_w[l],
        w_conv_out_bf, cos128 * scale, sin128 * scale, cos128, sin128)
    o, (w_attn_out_bf, w_o_bf, gate2_bf, up2_bf, down2_bf) = _diff_attn(
        q, k, vt, bias, lamv, subln_g[l].reshape(-1, 1),
        side_weights=[w_attn_out[l], w_o[l], ffn2_gate[l], ffn2_up[l], ffn2_down[l]])
    out = _merge_ffn(h, o.reshape(t, V_WIDTH), gy.reshape(t, d), ga.reshape(t, d),
                     w_attn_out_bf, w_o_bf, row(norm_ffn2[l]), gate2_bf, up2_bf, down2_bf,
                     row(norm_final))
    return out.reshape(bsz, seq, d)
```

```python
import math

import jax
import jax.numpy as jnp
from jax import lax
from jax.experimental import pallas as pl
from jax.experimental.pallas import tpu as pltpu

D_MODEL = 1024
CHUNK = 64
CONV_WIDTH = 512
CONV_K = 3
DA_HEADS = 4
DA_HEAD_DIM = 64
DA_V_DIM = 128
QK_WIDTH = 512
V_WIDTH = 512
D_FF = 2816
ROPE_THETA = 10000.0
NORM_EPS = 1e-6
LAMBDA_INIT = 0.8 - 0.6 * math.exp(-0.3 * 0)

OFF_XC, OFF_BG, OFF_CG, OFF_Q, OFF_K, OFF_V, OFF_GC, OFF_GA = (
    0, 512, 1024, 1536, 2048, 2560, 3072, 4096)
IN_WIDTH = 5120

V7X_VMEM_LIMIT_BYTES = 56 * 1024 * 1024
BF16_SUBLANES = 16

TM_FFN = 1024
TM_MERGE_FFN = 512
TM_PROJ = 1024
TQ = 512
TK = 512
HEADS_PER_STEP = 2
ATTN_COL_BLOCK = 512
CONV_HALO = 8

BF16 = jnp.bfloat16
F32 = jnp.float32


def _const_spec(shape):
    return pl.BlockSpec(shape, lambda *_: (0,) * len(shape), pipeline_mode=pl.Buffered(1))


def _rms(x, g):
    return x * lax.rsqrt(jnp.mean(x * x, axis=-1, keepdims=True) + NORM_EPS) * g


def _sigmoid(x):
    return 0.5 * jnp.tanh(0.5 * x) + 0.5


def _mm(a, b):
    return jnp.dot(a, b, preferred_element_type=F32)


def _mm_nt(a, b):
    return lax.dot_general(a, b, (((1,), (1,)), ((), ())), preferred_element_type=F32)


def _side_cast_specs(weights, n_steps, step_index):
    in_specs, out_specs, out_shapes = [], [], []
    for w in weights:
        rows, cols = w.shape
        assert rows % (n_steps * BF16_SUBLANES) == 0
        spec = pl.BlockSpec((rows // n_steps, cols), lambda *idx: (step_index(*idx), 0))
        in_specs.append(spec)
        out_specs.append(spec)
        out_shapes.append(jax.ShapeDtypeStruct(w.shape, BF16))
    return in_specs, out_specs, out_shapes


def _side_cast(src_refs, dst_refs):
    for src, dst in zip(src_refs, dst_refs):
        dst[...] = src[...].astype(BF16)


def _ffn_kernel(x_ref, g_ref, wg_ref, wu_ref, wd_ref, *rest):
    n_side = len(rest) // 2
    o_ref = rest[n_side]
    _side_cast(rest[:n_side], rest[n_side + 1:])
    x = x_ref[...]
    xn = _rms(x, g_ref[...]).astype(BF16)
    gate = _mm(xn, wg_ref[...])
    up = _mm(xn, wu_ref[...])
    a = (gate * _sigmoid(gate) * up).astype(BF16)
    o_ref[...] = x + 0.5 * _mm(a, wd_ref[...])


def _merge_ffn_kernel(h_ref, o_ref, gy_ref, ga_ref, wao_ref, wo_ref, g_ref, wg_ref, wu_ref,
                      wd_ref, gf_ref, out_ref):
    half = h_ref.shape[0] // 2
    rows = [slice(0, half), slice(half, 2 * half)]

    def merge(r):
        y_attn = _mm(o_ref[r, :], wao_ref[...])
        mix = gy_ref[r, :].astype(F32) + ga_ref[r, :].astype(F32) * y_attn
        return h_ref[r, :] + _mm(mix.astype(BF16), wo_ref[...])

    def gate_up(xn):
        return _mm(xn, wg_ref[...]), _mm(xn, wu_ref[...])

    def down(h2, gate, up):
        a = (gate * _sigmoid(gate) * up).astype(BF16)
        return h2 + 0.5 * _mm(a, wd_ref[...])

    norm = lambda x: _rms(x, g_ref[...]).astype(BF16)
    h2a = merge(rows[0])
    h2b = merge(rows[1])
    gu_a = gate_up(norm(h2a))
    ya = down(h2a, *gu_a)
    gu_b = gate_up(norm(h2b))
    out_ref[rows[0], :] = _rms(ya, gf_ref[...])
    yb = down(h2b, *gu_b)
    out_ref[rows[1], :] = _rms(yb, gf_ref[...])


_FFN_WEIGHT_SPECS = lambda: [
    _const_spec((1, D_MODEL)),
    _const_spec((D_MODEL, D_FF)),
    _const_spec((D_MODEL, D_FF)),
    _const_spec((D_FF, D_MODEL)),
]


def _ffn(x2d, g, wg, wu, wd, side_weights):
    t = x2d.shape[0]
    n_steps = t // TM_FFN
    tok = lambda i: (i, 0)
    side_in, side_out, side_shapes = _side_cast_specs(side_weights, n_steps, lambda i: i)
    outs = pl.pallas_call(
        _ffn_kernel,
        grid=(n_steps,),
        in_specs=[pl.BlockSpec((TM_FFN, D_MODEL), tok)] + _FFN_WEIGHT_SPECS() + side_in,
        out_specs=[pl.BlockSpec((TM_FFN, D_MODEL), tok)] + side_out,
        out_shape=[jax.ShapeDtypeStruct((t, D_MODEL), F32)] + side_shapes,
        name="ffn",
        compiler_params=pltpu.CompilerParams(
            dimension_semantics=("arbitrary",), vmem_limit_bytes=V7X_VMEM_LIMIT_BYTES),
    )(x2d, g, wg, wu, wd, *side_weights)
    return outs[0], outs[1:]


def _merge_ffn(h2d, o2d, gy2d, ga2d, wao, wo, g, wg, wu, wd, gf):
    t = h2d.shape[0]
    tm = TM_MERGE_FFN
    tok = lambda i: (i, 0)
    return pl.pallas_call(
        _merge_ffn_kernel,
        grid=(t // tm,),
        in_specs=[
            pl.BlockSpec((tm, D_MODEL), tok),
            pl.BlockSpec((tm, V_WIDTH), tok),
            pl.BlockSpec((tm, D_MODEL), tok),
            pl.BlockSpec((tm, D_MODEL), tok),
            _const_spec((V_WIDTH, D_MODEL)),
            _const_spec((D_MODEL, D_MODEL)),
        ] + _FFN_WEIGHT_SPECS() + [_const_spec((1, D_MODEL))],
        out_specs=pl.BlockSpec((tm, D_MODEL), tok),
        out_shape=jax.ShapeDtypeStruct((t, D_MODEL), F32),
        name="merge_ffn_final",
        compiler_params=pltpu.CompilerParams(
            dimension_semantics=("arbitrary",), vmem_limit_bytes=V7X_VMEM_LIMIT_BYTES),
    )(h2d, o2d, gy2d, ga2d, wao, wo, g, wg, wu, wd, gf)


def _rope(x, cos, sin_signed, lo_half):
    outs = []
    for c in range(x.shape[1] // 128):
        xc = x[:, c * 128:(c + 1) * 128]
        rot = jnp.where(lo_half, pltpu.roll(xc, 96, 1), pltpu.roll(xc, 32, 1))
        outs.append(xc * cos + rot * sin_signed)
    return jnp.concatenate(outs, axis=1)


def _in_proj_kernel(h_ref, g_ref, w_ref, bgate_ref, convw_ref, wco_ref, cosq_ref, sinq_ref,
                    cosk_ref, sink_ref, gy_ref, ga_ref, q_ref, k_ref, vt_ref, zbuf_ref):
    tm = h_ref.shape[1]

    @pl.when(pl.program_id(1) == 0)
    def _():
        zbuf_ref[0:CONV_HALO, :] = jnp.zeros((CONV_HALO, CONV_WIDTH), F32)

    u = _rms(h_ref[0], g_ref[...]).astype(BF16)

    def proj(off, width):
        return _mm(u, w_ref[:, off:off + width])

    vt_ref[0] = proj(OFF_V, V_WIDTH).T.astype(BF16)
    lane = lax.broadcasted_iota(jnp.int32, (tm, 128), 1)
    lo_half = (lane % DA_HEAD_DIM) < (DA_HEAD_DIM // 2)
    q_ref[0] = _rope(proj(OFF_Q, QK_WIDTH), cosq_ref[...], sinq_ref[...], lo_half).astype(BF16)
    k_ref[0] = _rope(proj(OFF_K, QK_WIDTH), cosk_ref[...], sink_ref[...], lo_half).astype(BF16)

    ga_ref[0] = _sigmoid(proj(OFF_GA, D_MODEL) + bgate_ref[:, D_MODEL:]).astype(ga_ref.dtype)
    g_conv = _sigmoid(proj(OFF_GC, D_MODEL) + bgate_ref[:, 0:D_MODEL])

    z = proj(OFF_CG, CONV_WIDTH) * proj(OFF_XC, CONV_WIDTH)
    zbuf_ref[CONV_HALO:CONV_HALO + tm, :] = z
    cw = convw_ref[...]
    zc = (cw[0:1, :] * zbuf_ref[CONV_HALO - 2:CONV_HALO - 2 + tm, :]
          + cw[1:2, :] * zbuf_ref[CONV_HALO - 1:CONV_HALO - 1 + tm, :]
          + cw[2:3, :] * z)
    zbuf_ref[0:CONV_HALO, :] = z[tm - CONV_HALO:, :]
    ya = _mm((proj(OFF_BG, CONV_WIDTH) * zc).astype(BF16), wco_ref[...])
    gy_ref[0] = (g_conv * ya).astype(gy_ref.dtype)


def _in_proj(h3d, g, w_in, b_gate, conv_w, w_conv_out, cosq, sinq, cosk, sink):
    b, s, _ = h3d.shape
    tm = TM_PROJ
    tok = lambda bi, si: (bi, si, 0)
    tab = lambda bi, si: (si, 0)
    bsd = lambda width, dt: jax.ShapeDtypeStruct((b, s, width), dt)
    return pl.pallas_call(
        _in_proj_kernel,
        grid=(b, s // tm),
        in_specs=[
            pl.BlockSpec((1, tm, D_MODEL), tok),
            _const_spec((1, D_MODEL)),
            _const_spec((D_MODEL, IN_WIDTH)),
            _const_spec((1, 2 * D_MODEL)),
            _const_spec((CONV_K, CONV_WIDTH)),
            _const_spec((CONV_WIDTH, D_MODEL)),
            pl.BlockSpec((tm, 128), tab),
            pl.BlockSpec((tm, 128), tab),
            pl.BlockSpec((tm, 128), tab),
            pl.BlockSpec((tm, 128), tab),
        ],
        out_specs=[
            pl.BlockSpec((1, tm, D_MODEL), tok),
            pl.BlockSpec((1, tm, D_MODEL), tok),
            pl.BlockSpec((1, tm, QK_WIDTH), tok),
            pl.BlockSpec((1, tm, QK_WIDTH), tok),
            pl.BlockSpec((1, V_WIDTH, tm), lambda bi, si: (bi, 0, si)),
        ],
        out_shape=[bsd(D_MODEL, BF16), bsd(D_MODEL, BF16), bsd(QK_WIDTH, BF16),
                   bsd(QK_WIDTH, BF16), jax.ShapeDtypeStruct((b, V_WIDTH, s), BF16)],
        scratch_shapes=[pltpu.VMEM((CONV_HALO + tm, CONV_WIDTH), F32)],
        name="in_proj",
        compiler_params=pltpu.CompilerParams(
            dimension_semantics=("arbitrary", "arbitrary"),
            vmem_limit_bytes=V7X_VMEM_LIMIT_BYTES),
    )(h3d, g, w_in, b_gate, conv_w, w_conv_out, cosq, sinq, cosk, sink)


def _attn_kernel(q_ref, k_ref, vt_ref, bias_ref, lamv_ref, sg_ref, *rest):
    n_side = (len(rest) - 5) // 2
    o_ref = rest[n_side]
    acc_ref, sa_ref, sb_ref, qqt_ref = rest[2 * n_side + 1:]
    _side_cast(rest[:n_side], rest[n_side + 1:2 * n_side + 1])
    tq = TQ
    n_qtiles = q_ref.shape[1] // tq
    heads = range(HEADS_PER_STEP)
    hcols = lambda hh: slice(hh * 128, (hh + 1) * 128)
    qrows = lambda qi: pl.ds(pl.multiple_of(qi * tq, tq), tq)

    feat = lax.broadcasted_iota(jnp.int32, (128, tq), 0)
    zero = jnp.zeros((128, tq), F32)
    ones = jnp.ones((BF16_SUBLANES, TK), BF16)
    lv = lamv_ref[...]
    lam = (jnp.exp(jnp.sum(lv[0:1] * lv[1:2], axis=-1, keepdims=True))
           - jnp.exp(jnp.sum(lv[2:3] * lv[3:4], axis=-1, keepdims=True)) + LAMBDA_INIT)
    out_gain = sg_ref[...] * (1.0 - LAMBDA_INIT)

    def stacked_qt(qi, hh):
        qt = q_ref[0, qrows(qi), hcols(hh)].astype(F32).T
        return jnp.concatenate([jnp.where(feat < DA_HEAD_DIM, qt, zero),
                                jnp.where(feat >= DA_HEAD_DIM, qt, zero)], axis=1).astype(BF16)

    cb = ATTN_COL_BLOCK
    subs = range(2 * tq // cb)

    def pieces(masked, sub):
        c0 = sub * cb
        if not masked:
            return [(slice(c0, c0 + cb), TK)]
        return [(slice(c0, c0 + cb // 2), TK // 2), (slice(c0 + cb // 2, c0 + cb), TK)]

    def scores(hh, tile, s_ref, masked, sub):
        start = pl.multiple_of(tile * TK, TK)
        maxes = []
        for cols, nk in pieces(masked, sub):
            s = _mm(k_ref[0, pl.ds(start, nk), hcols(hh)], qqt_ref[hh, :, cols])
            if masked:
                s = s + bias_ref[0:nk, cols.start % tq:(cols.start % tq) + cb // 2]
            s_ref[0:nk, cols] = s
            maxes.append(jnp.max(s, axis=0, keepdims=True))
        return jnp.concatenate(maxes, axis=1)

    def accumulate(hh, tile, s_ref, masked, s_max, m, sub):
        start = pl.multiple_of(tile * TK, TK)
        m_new = jnp.maximum(m, s_max)
        alpha = jnp.exp2(m - m_new)
        for cols, nk in pieces(masked, sub):
            blk = slice(cols.start - sub * cb, cols.stop - sub * cb)
            vt = jnp.concatenate([vt_ref[0, hcols(hh), pl.ds(start, nk)], ones[:, 0:nk]], axis=0)
            p = jnp.exp2(s_ref[0:nk, cols] - m_new[:, blk]).astype(BF16)
            acc_ref[hh, :, cols] = alpha[:, blk] * acc_ref[hh, :, cols] + _mm(vt, p)
        return m_new

    def phase(score_args, acc_args, s_maxes, ms):
        new_maxes, new_ms = [], []
        for sub in subs:
            if score_args is not None:
                new_maxes.append(scores(*score_args, sub))
            new_ms.append(accumulate(*acc_args, s_maxes[sub], ms[sub], sub))
        return tuple(new_maxes), tuple(new_ms)

    def finalize(qi, hh):
        o = acc_ref[hh, 0:DA_V_DIM, :] * (1.0 / acc_ref[hh, DA_V_DIM:DA_V_DIM + 1, :])
        o = o[:, :tq] - lam * o[:, tq:]
        o = o * lax.rsqrt(jnp.mean(o * o, axis=0, keepdims=True) + NORM_EPS) * out_gain
        o_ref[0, qrows(qi), hcols(hh)] = o.T.astype(o_ref.dtype)
        acc_ref[hh] = jnp.zeros(acc_ref.shape[1:], F32)

    m0 = (jnp.full((1, cb), -jnp.inf, F32),) * len(subs)

    def hand_over(qi, sb_max, mb):
        nxt = jnp.minimum(qi + 1, n_qtiles - 1)
        sa_max, _ = phase((0, nxt, sa_ref, True), (1, qi, sb_ref, True), sb_max, mb)
        qqt_ref[1] = stacked_qt(nxt, 1)
        finalize(qi, 1)
        return tuple(reversed(phase((1, 0, sb_ref, False), (0, nxt, sa_ref, True), sa_max, m0)))

    def q_tile(qi, carry):
        ma, sb_max = carry

        def body(j, carry):
            ma, mb, sb_max = carry
            sa_max, mb = phase((0, j, sa_ref, False), (1, j, sb_ref, False), sb_max, mb)
            sb_max, ma = phase((1, j + 1, sb_ref, False), (0, j, sa_ref, False), sa_max, ma)
            return ma, mb, sb_max

        n_loop = qi - 1
        carry = lax.fori_loop(0, n_loop // 2, lambda i, c: body(2 * i + 1, body(2 * i, c)),
                              (ma, m0, sb_max))
        ma, mb, sb_max = lax.cond(n_loop % 2 == 1, lambda c: body(n_loop - 1, c), lambda c: c,
                                  carry)
        sa_max, mb = phase((0, qi - 1, sa_ref, False), (1, qi - 1, sb_ref, False), sb_max, mb)
        sb_max, _ = phase((1, qi, sb_ref, True), (0, qi - 1, sa_ref, False), sa_max, ma)
        qqt_ref[0] = stacked_qt(jnp.minimum(qi + 1, n_qtiles - 1), 0)
        finalize(qi, 0)
        return hand_over(qi, sb_max, mb)

    acc_ref[...] = jnp.zeros_like(acc_ref)
    for hh in heads:
        qqt_ref[hh] = stacked_qt(0, hh)
    sa_max = tuple(scores(0, 0, sa_ref, True, sub) for sub in subs)
    sb_max, _ = phase((1, 0, sb_ref, True), (0, 0, sa_ref, True), sa_max, m0)
    qqt_ref[0] = stacked_qt(1, 0)
    finalize(0, 0)
    lax.fori_loop(1, n_qtiles, q_tile, hand_over(0, sb_max, m0))


def _diff_attn(q, k, vt, bias, lamv, subln_g_col, side_weights):
    b, s, _ = q.shape
    hw = 128 * HEADS_PER_STEP
    n_pairs = DA_HEADS // HEADS_PER_STEP
    side_in, side_out, side_shapes = _side_cast_specs(
        side_weights, b * n_pairs, lambda bi, hi: bi * n_pairs + hi)
    outs = pl.pallas_call(
        _attn_kernel,
        grid=(b, n_pairs),
        in_specs=[
            pl.BlockSpec((1, s, hw), lambda bi, hi: (bi, 0, hi)),
            pl.BlockSpec((1, s, hw), lambda bi, hi: (bi, 0, hi)),
            pl.BlockSpec((1, hw, s), lambda bi, hi: (bi, hi, 0)),
            _const_spec((TK, TQ)),
            _const_spec((4, DA_HEAD_DIM)),
            _const_spec((DA_V_DIM, 1)),
        ] + side_in,
        out_specs=[pl.BlockSpec((1, s, hw), lambda bi, hi: (bi, 0, hi))] + side_out,
        out_shape=[jax.ShapeDtypeStruct((b, s, V_WIDTH), BF16)] + side_shapes,
        scratch_shapes=[pltpu.VMEM((HEADS_PER_STEP, DA_V_DIM + BF16_SUBLANES, 2 * TQ), F32),
                        pltpu.VMEM((TK, 2 * TQ), F32),
                        pltpu.VMEM((TK, 2 * TQ), F32),
                        pltpu.VMEM((HEADS_PER_STEP, 128, 2 * TQ), BF16)],
        name="diff_attn",
        compiler_params=pltpu.CompilerParams(
            dimension_semantics=("arbitrary", "arbitrary"),
            vmem_limit_bytes=V7X_VMEM_LIMIT_BYTES),
    )(q, k, vt, bias, lamv, subln_g_col, *side_weights)
    return outs[0], outs[1:]


def _rope_tables(seq):
    half = DA_HEAD_DIM // 2
    inv_freq = 1.0 / (ROPE_THETA ** (jnp.arange(0, DA_HEAD_DIM, 2, dtype=F32) / DA_HEAD_DIM))
    ang = jnp.arange(seq, dtype=F32)[:, None] * inv_freq[None, :]
    cos, sin = jnp.cos(ang), jnp.sin(ang)
    cos128 = jnp.tile(cos, (1, 128 // half))
    sin128 = jnp.tile(jnp.concatenate([-sin, sin], axis=1), (1, 128 // DA_HEAD_DIM))
    return cos128, sin128


def kernel(x, norm_ffn1, ffn1_gate, ffn1_up, ffn1_down, norm_mix, w_in, b_gate, conv_w,
           w_conv_out, lambda_q1, lambda_k1, lambda_q2, lambda_k2, subln_g, w_attn_out, w_o,
           norm_ffn2, ffn2_gate, ffn2_up, ffn2_down, norm_final):
    bsz, seq, d = x.shape
    assert d == D_MODEL and norm_ffn1.shape[0] == 1
    assert seq % TQ == 0 and seq % TM_PROJ == 0
    assert (bsz * seq) % TM_FFN == 0 and (bsz * seq) % TM_MERGE_FFN == 0
    assert TQ == TK and TQ % CHUNK == 0
    assert HEADS_PER_STEP == 2 and DA_HEADS % HEADS_PER_STEP == 0
    assert ATTN_COL_BLOCK == TQ
    t = bsz * seq
    l = 0
    bf = lambda a: a.astype(BF16)
    row = lambda a: a.reshape(1, -1)

    cos128, sin128 = _rope_tables(seq)
    scale = math.log2(math.e) / math.sqrt(DA_HEAD_DIM)
    pos_chunk = jnp.arange(TQ, dtype=jnp.int32) // CHUNK
    bias = jnp.where(pos_chunk[:, None] <= pos_chunk[None, :], 0.0, -jnp.inf).astype(F32)
    lamv = jnp.stack([lambda_q1[l], lambda_k1[l], lambda_q2[l], lambda_k2[l]]).astype(F32)

    h, (w_in_bf, w_conv_out_bf) = _ffn(
        x.reshape(t, d), row(norm_ffn1[l]), bf(ffn1_gate[l]), bf(ffn1_up[l]), bf(ffn1_down[l]),
        side_weights=[w_in[l], w_conv_out[l]])
    gy, ga, q, k, vt = _in_proj(
        h.reshape(bsz, seq, d), row(norm_mix[l]), w_in_bf, row(b_gate[l]), conv_w[l],
        w_conv_out_bf, cos128 * scale, sin128 * scale, cos128, sin128)
    o, (w_attn_out_bf, w_o_bf, gate2_bf, up2_bf, down2_bf) = _diff_attn(
        q, k, vt, bias, lamv, subln_g[l].reshape(-1, 1),
        side_weights=[w_attn_out[l], w_o[l], ffn2_gate[l], ffn2_up[l], ffn2_down[l]])
    out = _merge_ffn(h, o.reshape(t, V_WIDTH), gy.reshape(t, d), ga.reshape(t, d),
                     w_attn_out_bf, w_o_bf, row(norm_ffn2[l]), gate2_bf, up2_bf, down2_bf,
                     row(norm_final))
    return out.reshape(bsz, seq, d)
```

```python
import math

import jax
import jax.numpy as jnp
from jax import lax
from jax.experimental import pallas as pl
from jax.experimental.pallas import tpu as pltpu

D_MODEL = 1024
CHUNK = 64
CONV_WIDTH = 512
CONV_K = 3
DA_HEADS = 4
DA_HEAD_DIM = 64
DA_V_DIM = 128
QK_WIDTH = 512
V_WIDTH = 512
D_FF = 2816
ROPE_THETA = 10000.0
NORM_EPS = 1e-6
LAMBDA_INIT = 0.8 - 0.6 * math.exp(-0.3 * 0)

OFF_XC, OFF_BG, OFF_CG, OFF_Q, OFF_K, OFF_V, OFF_GC, OFF_GA = (
    0, 512, 1024, 1536, 2048, 2560, 3072, 4096)
IN_WIDTH = 5120

V7X_VMEM_LIMIT_BYTES = 56 * 1024 * 1024
BF16_SUBLANES = 16

TM_FFN = 1024
TM_MERGE_FFN = 1024
MERGE_FFN_SLAB = 256
TM_PROJ = 1024
TQ = 512
TK = 512
HEADS_PER_STEP = 2
ATTN_COL_BLOCK = 512
CONV_HALO = 8

BF16 = jnp.bfloat16
F32 = jnp.float32


def _const_spec(shape):
    return pl.BlockSpec(shape, lambda *_: (0,) * len(shape), pipeline_mode=pl.Buffered(1))


def _rms(x, g):
    return x * lax.rsqrt(jnp.mean(x * x, axis=-1, keepdims=True) + NORM_EPS) * g


def _sigmoid(x):
    return 0.5 * jnp.tanh(0.5 * x) + 0.5


def _mm(a, b):
    return jnp.dot(a, b, preferred_element_type=F32)


def _mm_nt(a, b):
    return lax.dot_general(a, b, (((1,), (1,)), ((), ())), preferred_element_type=F32)


def _side_cast_specs(weights, n_steps, step_index):
    in_specs, out_specs, out_shapes = [], [], []
    for w in weights:
        rows, cols = w.shape
        assert rows % (n_steps * BF16_SUBLANES) == 0
        spec = pl.BlockSpec((rows // n_steps, cols), lambda *idx: (step_index(*idx), 0))
        in_specs.append(spec)
        out_specs.append(spec)
        out_shapes.append(jax.ShapeDtypeStruct(w.shape, BF16))
    return in_specs, out_specs, out_shapes


def _side_cast(src_refs, dst_refs):
    for src, dst in zip(src_refs, dst_refs):
        dst[...] = src[...].astype(BF16)


def _ffn_kernel(x_ref, g_ref, wg_ref, wu_ref, wd_ref, *rest):
    n_side = len(rest) // 2
    o_ref = rest[n_side]
    _side_cast(rest[:n_side], rest[n_side + 1:])
    x = x_ref[...]
    xn = _rms(x, g_ref[...]).astype(BF16)
    gate = _mm(xn, wg_ref[...])
    up = _mm(xn, wu_ref[...])
    a = (gate * _sigmoid(gate) * up).astype(BF16)
    o_ref[...] = x + 0.5 * _mm(a, wd_ref[...])


def _merge_ffn_kernel(h_ref, o_ref, gy_ref, ga_ref, wao_ref, wo_ref, g_ref, wg_ref, wu_ref,
                      wd_ref, gf_ref, out_ref):
    slab = MERGE_FFN_SLAB
    n_slabs = h_ref.shape[0] // slab

    def merge(r):
        y_attn = _mm(o_ref[r, :], wao_ref[...])
        mix = gy_ref[r, :].astype(F32) + ga_ref[r, :].astype(F32) * y_attn
        return h_ref[r, :] + _mm(mix.astype(BF16), wo_ref[...])

    def gate_up(xn):
        return _mm(xn, wg_ref[...]), _mm(xn, wu_ref[...])

    def down(h2, gate, up):
        a = (gate * _sigmoid(gate) * up).astype(BF16)
        return h2 + 0.5 * _mm(a, wd_ref[...])

    norm = lambda x: _rms(x, g_ref[...]).astype(BF16)
    for pair in range(n_slabs // 2):
        ra, rb = (slice(s * slab, (s + 1) * slab) for s in (2 * pair, 2 * pair + 1))
        h2a = merge(ra)
        h2b = merge(rb)
        gu_a = gate_up(norm(h2a))
        ya = down(h2a, *gu_a)
        gu_b = gate_up(norm(h2b))
        out_ref[ra, :] = _rms(ya, gf_ref[...])
        yb = down(h2b, *gu_b)
        out_ref[rb, :] = _rms(yb, gf_ref[...])


_FFN_WEIGHT_SPECS = lambda: [
    _const_spec((1, D_MODEL)),
    _const_spec((D_MODEL, D_FF)),
    _const_spec((D_MODEL, D_FF)),
    _const_spec((D_FF, D_MODEL)),
]


def _ffn(x2d, g, wg, wu, wd, side_weights):
    t = x2d.shape[0]
    n_steps = t // TM_FFN
    tok = lambda i: (i, 0)
    side_in, side_out, side_shapes = _side_cast_specs(side_weights, n_steps, lambda i: i)
    outs = pl.pallas_call(
        _ffn_kernel,
        grid=(n_steps,),
        in_specs=[pl.BlockSpec((TM_FFN, D_MODEL), tok)] + _FFN_WEIGHT_SPECS() + side_in,
        out_specs=[pl.BlockSpec((TM_FFN, D_MODEL), tok)] + side_out,
        out_shape=[jax.ShapeDtypeStruct((t, D_MODEL), F32)] + side_shapes,
        name="ffn",
        compiler_params=pltpu.CompilerParams(
            dimension_semantics=("arbitrary",), vmem_limit_bytes=V7X_VMEM_LIMIT_BYTES),
    )(x2d, g, wg, wu, wd, *side_weights)
    return outs[0], outs[1:]


def _merge_ffn(h2d, o2d, gy2d, ga2d, wao, wo, g, wg, wu, wd, gf):
    t = h2d.shape[0]
    tm = TM_MERGE_FFN
    tok = lambda i: (i, 0)
    return pl.pallas_call(
        _merge_ffn_kernel,
        grid=(t // tm,),
        in_specs=[
            pl.BlockSpec((tm, D_MODEL), tok),
            pl.BlockSpec((tm, V_WIDTH), tok),
            pl.BlockSpec((tm, D_MODEL), tok),
            pl.BlockSpec((tm, D_MODEL), tok),
            _const_spec((V_WIDTH, D_MODEL)),
            _const_spec((D_MODEL, D_MODEL)),
        ] + _FFN_WEIGHT_SPECS() + [_const_spec((1, D_MODEL))],
        out_specs=pl.BlockSpec((tm, D_MODEL), tok),
        out_shape=jax.ShapeDtypeStruct((t, D_MODEL), F32),
        name="merge_ffn_final",
        compiler_params=pltpu.CompilerParams(
            dimension_semantics=("arbitrary",), vmem_limit_bytes=V7X_VMEM_LIMIT_BYTES),
    )(h2d, o2d, gy2d, ga2d, wao, wo, g, wg, wu, wd, gf)


def _rope(x, cos, sin_signed, lo_half):
    outs = []
    for c in range(x.shape[1] // 128):
        xc = x[:, c * 128:(c + 1) * 128]
        rot = jnp.where(lo_half, pltpu.roll(xc, 96, 1), pltpu.roll(xc, 32, 1))
        outs.append(xc * cos + rot * sin_signed)
    return jnp.concatenate(outs, axis=1)


def _in_proj_kernel(h_ref, g_ref, w_ref, bgate_ref, convw_ref, wco_ref, cosq_ref, sinq_ref,
                    cosk_ref, sink_ref, gy_ref, ga_ref, q_ref, k_ref, vt_ref, zbuf_ref):
    tm = h_ref.shape[1]

    @pl.when(pl.program_id(1) == 0)
    def _():
        zbuf_ref[0:CONV_HALO, :] = jnp.zeros((CONV_HALO, CONV_WIDTH), F32)

    u = _rms(h_ref[0], g_ref[...]).astype(BF16)

    def proj(off, width):
        return _mm(u, w_ref[:, off:off + width])

    vt_ref[0] = proj(OFF_V, V_WIDTH).T.astype(BF16)
    lane = lax.broadcasted_iota(jnp.int32, (tm, 128), 1)
    lo_half = (lane % DA_HEAD_DIM) < (DA_HEAD_DIM // 2)
    q_ref[0] = _rope(proj(OFF_Q, QK_WIDTH), cosq_ref[...], sinq_ref[...], lo_half).astype(BF16)
    k_ref[0] = _rope(proj(OFF_K, QK_WIDTH), cosk_ref[...], sink_ref[...], lo_half).astype(BF16)

    ga_ref[0] = _sigmoid(proj(OFF_GA, D_MODEL) + bgate_ref[:, D_MODEL:]).astype(ga_ref.dtype)
    g_conv = _sigmoid(proj(OFF_GC, D_MODEL) + bgate_ref[:, 0:D_MODEL])

    z = proj(OFF_CG, CONV_WIDTH) * proj(OFF_XC, CONV_WIDTH)
    zbuf_ref[CONV_HALO:CONV_HALO + tm, :] = z
    cw = convw_ref[...]
    zc = (cw[0:1, :] * zbuf_ref[CONV_HALO - 2:CONV_HALO - 2 + tm, :]
          + cw[1:2, :] * zbuf_ref[CONV_HALO - 1:CONV_HALO - 1 + tm, :]
          + cw[2:3, :] * z)
    zbuf_ref[0:CONV_HALO, :] = z[tm - CONV_HALO:, :]
    ya = _mm((proj(OFF_BG, CONV_WIDTH) * zc).astype(BF16), wco_ref[...])
    gy_ref[0] = (g_conv * ya).astype(gy_ref.dtype)


def _in_proj(h3d, g, w_in, b_gate, conv_w, w_conv_out, cosq, sinq, cosk, sink):
    b, s, _ = h3d.shape
    tm = TM_PROJ
    tok = lambda bi, si: (bi, si, 0)
    tab = lambda bi, si: (si, 0)
    bsd = lambda width, dt: jax.ShapeDtypeStruct((b, s, width), dt)
    return pl.pallas_call(
        _in_proj_kernel,
        grid=(b, s // tm),
        in_specs=[
            pl.BlockSpec((1, tm, D_MODEL), tok),
            _const_spec((1, D_MODEL)),
            _const_spec((D_MODEL, IN_WIDTH)),
            _const_spec((1, 2 * D_MODEL)),
            _const_spec((CONV_K, CONV_WIDTH)),
            _const_spec((CONV_WIDTH, D_MODEL)),
            pl.BlockSpec((tm, 128), tab),
            pl.BlockSpec((tm, 128), tab),
            pl.BlockSpec((tm, 128), tab),
            pl.BlockSpec((tm, 128), tab),
        ],
        out_specs=[
            pl.BlockSpec((1, tm, D_MODEL), tok),
            pl.BlockSpec((1, tm, D_MODEL), tok),
            pl.BlockSpec((1, tm, QK_WIDTH), tok),
            pl.BlockSpec((1, tm, QK_WIDTH), tok),
            pl.BlockSpec((1, V_WIDTH, tm), lambda bi, si: (bi, 0, si)),
        ],
        out_shape=[bsd(D_MODEL, BF16), bsd(D_MODEL, BF16), bsd(QK_WIDTH, BF16),
                   bsd(QK_WIDTH, BF16), jax.ShapeDtypeStruct((b, V_WIDTH, s), BF16)],
        scratch_shapes=[pltpu.VMEM((CONV_HALO + tm, CONV_WIDTH), F32)],
        name="in_proj",
        compiler_params=pltpu.CompilerParams(
            dimension_semantics=("arbitrary", "arbitrary"),
            vmem_limit_bytes=V7X_VMEM_LIMIT_BYTES),
    )(h3d, g, w_in, b_gate, conv_w, w_conv_out, cosq, sinq, cosk, sink)


def _attn_kernel(q_ref, k_ref, vt_ref, bias_ref, lamv_ref, sg_ref, *rest):
    n_side = (len(rest) - 5) // 2
    o_ref = rest[n_side]
    acc_ref, sa_ref, sb_ref, qqt_ref = rest[2 * n_side + 1:]
    _side_cast(rest[:n_side], rest[n_side + 1:2 * n_side + 1])
    tq = TQ
    n_qtiles = q_ref.shape[1] // tq
    heads = range(HEADS_PER_STEP)
    hcols = lambda hh: slice(hh * 128, (hh + 1) * 128)
    qrows = lambda qi: pl.ds(pl.multiple_of(qi * tq, tq), tq)

    feat = lax.broadcasted_iota(jnp.int32, (128, tq), 0)
    zero = jnp.zeros((128, tq), F32)
    ones = jnp.ones((BF16_SUBLANES, TK), BF16)
    lv = lamv_ref[...]
    lam = (jnp.exp(jnp.sum(lv[0:1] * lv[1:2], axis=-1, keepdims=True))
           - jnp.exp(jnp.sum(lv[2:3] * lv[3:4], axis=-1, keepdims=True)) + LAMBDA_INIT)
    out_gain = sg_ref[...] * (1.0 - LAMBDA_INIT)

    def stacked_qt(qi, hh):
        qt = q_ref[0, qrows(qi), hcols(hh)].astype(F32).T
        return jnp.concatenate([jnp.where(feat < DA_HEAD_DIM, qt, zero),
                                jnp.where(feat >= DA_HEAD_DIM, qt, zero)], axis=1).astype(BF16)

    cb = ATTN_COL_BLOCK
    subs = range(2 * tq // cb)

    def pieces(masked, sub):
        c0 = sub * cb
        if not masked:
            return [(slice(c0, c0 + cb), TK)]
        return [(slice(c0, c0 + cb // 2), TK // 2), (slice(c0 + cb // 2, c0 + cb), TK)]

    def scores(hh, tile, s_ref, masked, sub):
        start = pl.multiple_of(tile * TK, TK)
        maxes = []
        for cols, nk in pieces(masked, sub):
            s = _mm(k_ref[0, pl.ds(start, nk), hcols(hh)], qqt_ref[hh, :, cols])
            if masked:
                s = s + bias_ref[0:nk, cols.start % tq:(cols.start % tq) + cb // 2]
            s_ref[0:nk, cols] = s
            maxes.append(jnp.max(s, axis=0, keepdims=True))
        return jnp.concatenate(maxes, axis=1)

    def accumulate(hh, tile, s_ref, masked, s_max, m, sub):
        start = pl.multiple_of(tile * TK, TK)
        m_new = jnp.maximum(m, s_max)
        alpha = jnp.exp2(m - m_new)
        for cols, nk in pieces(masked, sub):
            blk = slice(cols.start - sub * cb, cols.stop - sub * cb)
            vt = jnp.concatenate([vt_ref[0, hcols(hh), pl.ds(start, nk)], ones[:, 0:nk]], axis=0)
            p = jnp.exp2(s_ref[0:nk, cols] - m_new[:, blk]).astype(BF16)
            acc_ref[hh, :, cols] = alpha[:, blk] * acc_ref[hh, :, cols] + _mm(vt, p)
        return m_new

    def phase(score_args, acc_args, s_maxes, ms):
        new_maxes, new_ms = [], []
        for sub in subs:
            if score_args is not None:
                new_maxes.append(scores(*score_args, sub))
            new_ms.append(accumulate(*acc_args, s_maxes[sub], ms[sub], sub))
        return tuple(new_maxes), tuple(new_ms)

    def finalize(qi, hh):
        o = acc_ref[hh, 0:DA_V_DIM, :] * (1.0 / acc_ref[hh, DA_V_DIM:DA_V_DIM + 1, :])
        o = o[:, :tq] - lam * o[:, tq:]
        o = o * lax.rsqrt(jnp.mean(o * o, axis=0, keepdims=True) + NORM_EPS) * out_gain
        o_ref[0, qrows(qi), hcols(hh)] = o.T.astype(o_ref.dtype)
        acc_ref[hh] = jnp.zeros(acc_ref.shape[1:], F32)

    m0 = (jnp.full((1, cb), -jnp.inf, F32),) * len(subs)

    def hand_over(qi, sb_max, mb):
        nxt = jnp.minimum(qi + 1, n_qtiles - 1)
        sa_max, _ = phase((0, nxt, sa_ref, True), (1, qi, sb_ref, True), sb_max, mb)
        qqt_ref[1] = stacked_qt(nxt, 1)
        finalize(qi, 1)
        return tuple(reversed(phase((1, 0, sb_ref, False), (0, nxt, sa_ref, True), sa_max, m0)))

    def q_tile(qi, carry):
        ma, sb_max = carry

        def body(j, carry):
            ma, mb, sb_max = carry
            sa_max, mb = phase((0, j, sa_ref, False), (1, j, sb_ref, False), sb_max, mb)
            sb_max, ma = phase((1, j + 1, sb_ref, False), (0, j, sa_ref, False), sa_max, ma)
            return ma, mb, sb_max

        n_loop = qi - 1
        carry = lax.fori_loop(0, n_loop // 2, lambda i, c: body(2 * i + 1, body(2 * i, c)),
                              (ma, m0, sb_max))
        ma, mb, sb_max = lax.cond(n_loop % 2 == 1, lambda c: body(n_loop - 1, c), lambda c: c,
                                  carry)
        sa_max, mb = phase((0, qi - 1, sa_ref, False), (1, qi - 1, sb_ref, False), sb_max, mb)
        sb_max, _ = phase((1, qi, sb_ref, True), (0, qi - 1, sa_ref, False), sa_max, ma)
        qqt_ref[0] = stacked_qt(jnp.minimum(qi + 1, n_qtiles - 1), 0)
        finalize(qi, 0)
        return hand_over(qi, sb_max, mb)

    acc_ref[...] = jnp.zeros_like(acc_ref)
    for hh in heads:
        qqt_ref[hh] = stacked_qt(0, hh)
    sa_max = tuple(scores(0, 0, sa_ref, True, sub) for sub in subs)
    sb_max, _ = phase((1, 0, sb_ref, True), (0, 0, sa_ref, True), sa_max, m0)
    qqt_ref[0] = stacked_qt(1, 0)
    finalize(0, 0)
    lax.fori_loop(1, n_qtiles, q_tile, hand_over(0, sb_max, m0))


def _diff_attn(q, k, vt, bias, lamv, subln_g_col, side_weights):
    b, s, _ = q.shape
    hw = 128 * HEADS_PER_STEP
    n_pairs = DA_HEADS // HEADS_PER_STEP
    side_in, side_out, side_shapes = _side_cast_specs(
        side_weights, b * n_pairs, lambda bi, hi: bi * n_pairs + hi)
    outs = pl.pallas_call(
        _attn_kernel,
        grid=(b, n_pairs),
        in_specs=[
            pl.BlockSpec((1, s, hw), lambda bi, hi: (bi, 0, hi)),
            pl.BlockSpec((1, s, hw), lambda bi, hi: (bi, 0, hi)),
            pl.BlockSpec((1, hw, s), lambda bi, hi: (bi, hi, 0)),
            _const_spec((TK, TQ)),
            _const_spec((4, DA_HEAD_DIM)),
            _const_spec((DA_V_DIM, 1)),
        ] + side_in,
        out_specs=[pl.BlockSpec((1, s, hw), lambda bi, hi: (bi, 0, hi))] + side_out,
        out_shape=[jax.ShapeDtypeStruct((b, s, V_WIDTH), BF16)] + side_shapes,
        scratch_shapes=[pltpu.VMEM((HEADS_PER_STEP, DA_V_DIM + BF16_SUBLANES, 2 * TQ), F32),
                        pltpu.VMEM((TK, 2 * TQ), F32),
                        pltpu.VMEM((TK, 2 * TQ), F32),
                        pltpu.VMEM((HEADS_PER_STEP, 128, 2 * TQ), BF16)],
        name="diff_attn",
        compiler_params=pltpu.CompilerParams(
            dimension_semantics=("arbitrary", "arbitrary"),
            vmem_limit_bytes=V7X_VMEM_LIMIT_BYTES),
    )(q, k, vt, bias, lamv, subln_g_col, *side_weights)
    return outs[0], outs[1:]


def _rope_tables(seq):
    half = DA_HEAD_DIM // 2
    inv_freq = 1.0 / (ROPE_THETA ** (jnp.arange(0, DA_HEAD_DIM, 2, dtype=F32) / DA_HEAD_DIM))
    ang = jnp.arange(seq, dtype=F32)[:, None] * inv_freq[None, :]
    cos, sin = jnp.cos(ang), jnp.sin(ang)
    cos128 = jnp.tile(cos, (1, 128 // half))
    sin128 = jnp.tile(jnp.concatenate([-sin, sin], axis=1), (1, 128 // DA_HEAD_DIM))
    return cos128, sin128


def kernel(x, norm_ffn1, ffn1_gate, ffn1_up, ffn1_down, norm_mix, w_in, b_gate, conv_w,
           w_conv_out, lambda_q1, lambda_k1, lambda_q2, lambda_k2, subln_g, w_attn_out, w_o,
           norm_ffn2, ffn2_gate, ffn2_up, ffn2_down, norm_final):
    bsz, seq, d = x.shape
    assert d == D_MODEL and norm_ffn1.shape[0] == 1
    assert seq % TQ == 0 and seq % TM_PROJ == 0
    assert (bsz * seq) % TM_FFN == 0 and (bsz * seq) % TM_MERGE_FFN == 0
    assert TQ == TK and TQ % CHUNK == 0
    assert HEADS_PER_STEP == 2 and DA_HEADS % HEADS_PER_STEP == 0
    assert ATTN_COL_BLOCK == TQ
    t = bsz * seq
    l = 0
    bf = lambda a: a.astype(BF16)
    row = lambda a: a.reshape(1, -1)

    cos128, sin128 = _rope_tables(seq)
    scale = math.log2(math.e) / math.sqrt(DA_HEAD_DIM)
    pos_chunk = jnp.arange(TQ, dtype=jnp.int32) // CHUNK
    bias = jnp.where(pos_chunk[:, None] <= pos_chunk[None, :], 0.0, -jnp.inf).astype(F32)
    lamv = jnp.stack([lambda_q1[l], lambda_k1[l], lambda_q2[l], lambda_k2[l]]).astype(F32)

    h, (w_in_bf, w_conv_out_bf) = _ffn(
        x.reshape(t, d), row(norm_ffn1[l]), bf(ffn1_gate[l]), bf(ffn1_up[l]), bf(ffn1_down[l]),
        side_weights=[w_in[l], w_conv_out[l]])
    gy, ga, q, k, vt = _in_proj(
        h.reshape(bsz, seq, d), row(norm_mix[l]), w_in_bf, row(b_gate[l]), conv_w[l],
        w_conv_out_bf, cos128 * scale, sin128 * scale, cos128, sin128)
    o, (w_attn_out_bf, w_o_bf, gate2_bf, up2_bf, down2_bf) = _diff_attn(
        q, k, vt, bias, lamv, subln_g[l].reshape(-1, 1),
        side_weights=[w_attn_out[l], w_o[l], ffn2_gate[l], ffn2_up[l], ffn2_down[l]])
    out = _merge_ffn(h, o.reshape(t, V_WIDTH), gy.reshape(t, d), ga.reshape(t, d),
                     w_attn_out_bf, w_o_bf, row(norm_ffn2[l]), gate2_bf, up2_bf, down2_bf,
                     row(norm_final))
    return out.reshape(bsz, seq, d)
```

```python
import math

import jax
import jax.numpy as jnp
import numpy as np
from jax import lax
from jax.experimental import pallas as pl
from jax.experimental.pallas import tpu as pltpu

D_MODEL = 1024
CHUNK = 64
CONV_WIDTH = 512
CONV_K = 3
DA_HEADS = 4
DA_HEAD_DIM = 64
DA_V_DIM = 128
QK_WIDTH = 512
V_WIDTH = 512
D_FF = 2816
ROPE_THETA = 10000.0
NORM_EPS = 1e-6
LAMBDA_INIT = 0.8 - 0.6 * math.exp(-0.3 * 0)

OFF_XC, OFF_BG, OFF_CG, OFF_Q, OFF_K, OFF_V, OFF_GC, OFF_GA = (
    0, 512, 1024, 1536, 2048, 2560, 3072, 4096)
IN_WIDTH = 5120

V7X_VMEM_LIMIT_BYTES = 56 * 1024 * 1024
BF16_SUBLANES = 16

TM_FFN = 1024
TM_MERGE_FFN = 1024
MERGE_FFN_SLAB = 256
TM_PROJ = 1024
TQ = 512
TK = 512
HEADS_PER_STEP = 2
ATTN_COL_BLOCK = 512
CONV_HALO = 8

BF16 = jnp.bfloat16
F32 = jnp.float32


def _const_spec(shape):
    return pl.BlockSpec(shape, lambda *_: (0,) * len(shape), pipeline_mode=pl.Buffered(1))


def _rms(x, g):
    return x * lax.rsqrt(jnp.mean(x * x, axis=-1, keepdims=True) + NORM_EPS) * g


def _sigmoid(x):
    return 0.5 * jnp.tanh(0.5 * x) + 0.5


def _mm(a, b):
    return jnp.dot(a, b, preferred_element_type=F32)


def _mm_nt(a, b):
    return lax.dot_general(a, b, (((1,), (1,)), ((), ())), preferred_element_type=F32)


def _side_cast_specs(weights, n_steps, step_index):
    in_specs, out_specs, out_shapes = [], [], []
    for w in weights:
        rows, cols = w.shape
        assert rows % (n_steps * BF16_SUBLANES) == 0
        spec = pl.BlockSpec((rows // n_steps, cols), lambda *idx: (step_index(*idx), 0))
        in_specs.append(spec)
        out_specs.append(spec)
        out_shapes.append(jax.ShapeDtypeStruct(w.shape, BF16))
    return in_specs, out_specs, out_shapes


def _side_cast(src_refs, dst_refs):
    for src, dst in zip(src_refs, dst_refs):
        dst[...] = src[...].astype(BF16)


def _ffn_kernel(x_ref, g_ref, wg_ref, wu_ref, wd_ref, *rest):
    n_side = len(rest) // 2
    o_ref = rest[n_side]
    _side_cast(rest[:n_side], rest[n_side + 1:])
    x = x_ref[...]
    xn = _rms(x, g_ref[...]).astype(BF16)
    gate = _mm(xn, wg_ref[...])
    up = _mm(xn, wu_ref[...])
    a = (gate * _sigmoid(gate) * up).astype(BF16)
    o_ref[...] = x + 0.5 * _mm(a, wd_ref[...])


def _merge_ffn_kernel(h_ref, o_ref, gy_ref, ga_ref, wao_ref, wo_ref, g_ref, wg_ref, wu_ref,
                      wd_ref, gf_ref, out_ref):
    slab = MERGE_FFN_SLAB
    n_slabs = h_ref.shape[0] // slab

    def merge(r):
        y_attn = _mm(o_ref[r, :], wao_ref[...])
        mix = gy_ref[r, :].astype(F32) + ga_ref[r, :].astype(F32) * y_attn
        return h_ref[r, :] + _mm(mix.astype(BF16), wo_ref[...])

    def gate_up(xn):
        return _mm(xn, wg_ref[...]), _mm(xn, wu_ref[...])

    def down(h2, gate, up):
        a = (gate * _sigmoid(gate) * up).astype(BF16)
        return h2 + 0.5 * _mm(a, wd_ref[...])

    norm = lambda x: _rms(x, g_ref[...]).astype(BF16)
    for pair in range(n_slabs // 2):
        ra, rb = (slice(s * slab, (s + 1) * slab) for s in (2 * pair, 2 * pair + 1))
        h2a = merge(ra)
        h2b = merge(rb)
        gu_a = gate_up(norm(h2a))
        ya = down(h2a, *gu_a)
        gu_b = gate_up(norm(h2b))
        out_ref[ra, :] = _rms(ya, gf_ref[...])
        yb = down(h2b, *gu_b)
        out_ref[rb, :] = _rms(yb, gf_ref[...])


_FFN_WEIGHT_SPECS = lambda: [
    _const_spec((1, D_MODEL)),
    _const_spec((D_MODEL, D_FF)),
    _const_spec((D_MODEL, D_FF)),
    _const_spec((D_FF, D_MODEL)),
]


def _ffn(x2d, g, wg, wu, wd, side_weights):
    t = x2d.shape[0]
    n_steps = t // TM_FFN
    tok = lambda i: (i, 0)
    side_in, side_out, side_shapes = _side_cast_specs(side_weights, n_steps, lambda i: i)
    outs = pl.pallas_call(
        _ffn_kernel,
        grid=(n_steps,),
        in_specs=[pl.BlockSpec((TM_FFN, D_MODEL), tok)] + _FFN_WEIGHT_SPECS() + side_in,
        out_specs=[pl.BlockSpec((TM_FFN, D_MODEL), tok)] + side_out,
        out_shape=[jax.ShapeDtypeStruct((t, D_MODEL), F32)] + side_shapes,
        name="ffn",
        compiler_params=pltpu.CompilerParams(
            dimension_semantics=("arbitrary",), vmem_limit_bytes=V7X_VMEM_LIMIT_BYTES),
    )(x2d, g, wg, wu, wd, *side_weights)
    return outs[0], outs[1:]


def _merge_ffn(h2d, o2d, gy2d, ga2d, wao, wo, g, wg, wu, wd, gf):
    t = h2d.shape[0]
    tm = TM_MERGE_FFN
    tok = lambda i: (i, 0)
    return pl.pallas_call(
        _merge_ffn_kernel,
        grid=(t // tm,),
        in_specs=[
            pl.BlockSpec((tm, D_MODEL), tok),
            pl.BlockSpec((tm, V_WIDTH), tok),
            pl.BlockSpec((tm, D_MODEL), tok),
            pl.BlockSpec((tm, D_MODEL), tok),
            _const_spec((V_WIDTH, D_MODEL)),
            _const_spec((D_MODEL, D_MODEL)),
        ] + _FFN_WEIGHT_SPECS() + [_const_spec((1, D_MODEL))],
        out_specs=pl.BlockSpec((tm, D_MODEL), tok),
        out_shape=jax.ShapeDtypeStruct((t, D_MODEL), F32),
        name="merge_ffn_final",
        compiler_params=pltpu.CompilerParams(
            dimension_semantics=("arbitrary",), vmem_limit_bytes=V7X_VMEM_LIMIT_BYTES),
    )(h2d, o2d, gy2d, ga2d, wao, wo, g, wg, wu, wd, gf)


def _rope(x, cos, sin_signed, lo_half):
    outs = []
    for c in range(x.shape[1] // 128):
        xc = x[:, c * 128:(c + 1) * 128]
        rot = jnp.where(lo_half, pltpu.roll(xc, 96, 1), pltpu.roll(xc, 32, 1))
        outs.append(xc * cos + rot * sin_signed)
    return jnp.concatenate(outs, axis=1)


def _in_proj_kernel(h_ref, g_ref, w_ref, bgate_ref, convw_ref, wco_ref, cosq_ref, sinq_ref,
                    cosk_ref, sink_ref, gy_ref, ga_ref, q_ref, k_ref, vt_ref, zbuf_ref):
    tm = h_ref.shape[1]

    @pl.when(pl.program_id(1) == 0)
    def _():
        zbuf_ref[0:CONV_HALO, :] = jnp.zeros((CONV_HALO, CONV_WIDTH), F32)

    u = _rms(h_ref[0], g_ref[...]).astype(BF16)

    def proj(off, width):
        return _mm(u, w_ref[:, off:off + width])

    vt_ref[0] = proj(OFF_V, V_WIDTH).T.astype(BF16)
    lane = lax.broadcasted_iota(jnp.int32, (tm, 128), 1)
    lo_half = (lane % DA_HEAD_DIM) < (DA_HEAD_DIM // 2)
    q_ref[0] = _rope(proj(OFF_Q, QK_WIDTH), cosq_ref[...], sinq_ref[...], lo_half).astype(BF16)
    k_ref[0] = _rope(proj(OFF_K, QK_WIDTH), cosk_ref[...], sink_ref[...], lo_half).astype(BF16)

    ga_ref[0] = _sigmoid(proj(OFF_GA, D_MODEL) + bgate_ref[:, D_MODEL:]).astype(ga_ref.dtype)
    g_conv = _sigmoid(proj(OFF_GC, D_MODEL) + bgate_ref[:, 0:D_MODEL])

    z = proj(OFF_CG, CONV_WIDTH) * proj(OFF_XC, CONV_WIDTH)
    zbuf_ref[CONV_HALO:CONV_HALO + tm, :] = z
    cw = convw_ref[...]
    zc = (cw[0:1, :] * zbuf_ref[CONV_HALO - 2:CONV_HALO - 2 + tm, :]
          + cw[1:2, :] * zbuf_ref[CONV_HALO - 1:CONV_HALO - 1 + tm, :]
          + cw[2:3, :] * z)
    zbuf_ref[0:CONV_HALO, :] = z[tm - CONV_HALO:, :]
    ya = _mm((proj(OFF_BG, CONV_WIDTH) * zc).astype(BF16), wco_ref[...])
    gy_ref[0] = (g_conv * ya).astype(gy_ref.dtype)


def _in_proj(h3d, g, w_in, b_gate, conv_w, w_conv_out, cosq, sinq, cosk, sink):
    b, s, _ = h3d.shape
    tm = TM_PROJ
    tok = lambda bi, si: (bi, si, 0)
    tab = lambda bi, si: (si, 0)
    bsd = lambda width, dt: jax.ShapeDtypeStruct((b, s, width), dt)
    return pl.pallas_call(
        _in_proj_kernel,
        grid=(b, s // tm),
        in_specs=[
            pl.BlockSpec((1, tm, D_MODEL), tok),
            _const_spec((1, D_MODEL)),
            _const_spec((D_MODEL, IN_WIDTH)),
            _const_spec((1, 2 * D_MODEL)),
            _const_spec((CONV_K, CONV_WIDTH)),
            _const_spec((CONV_WIDTH, D_MODEL)),
            pl.BlockSpec((tm, 128), tab),
            pl.BlockSpec((tm, 128), tab),
            pl.BlockSpec((tm, 128), tab),
            pl.BlockSpec((tm, 128), tab),
        ],
        out_specs=[
            pl.BlockSpec((1, tm, D_MODEL), tok),
            pl.BlockSpec((1, tm, D_MODEL), tok),
            pl.BlockSpec((1, tm, QK_WIDTH), tok),
            pl.BlockSpec((1, tm, QK_WIDTH), tok),
            pl.BlockSpec((1, V_WIDTH, tm), lambda bi, si: (bi, 0, si)),
        ],
        out_shape=[bsd(D_MODEL, BF16), bsd(D_MODEL, BF16), bsd(QK_WIDTH, BF16),
                   bsd(QK_WIDTH, BF16), jax.ShapeDtypeStruct((b, V_WIDTH, s), BF16)],
        scratch_shapes=[pltpu.VMEM((CONV_HALO + tm, CONV_WIDTH), F32)],
        name="in_proj",
        compiler_params=pltpu.CompilerParams(
            dimension_semantics=("arbitrary", "arbitrary"),
            vmem_limit_bytes=V7X_VMEM_LIMIT_BYTES),
    )(h3d, g, w_in, b_gate, conv_w, w_conv_out, cosq, sinq, cosk, sink)


def _attn_kernel(q_ref, k_ref, vt_ref, bias_ref, lamv_ref, sg_ref, *rest):
    n_side = (len(rest) - 5) // 2
    o_ref = rest[n_side]
    acc_ref, sa_ref, sb_ref, qqt_ref = rest[2 * n_side + 1:]
    _side_cast(rest[:n_side], rest[n_side + 1:2 * n_side + 1])
    tq = TQ
    n_qtiles = q_ref.shape[1] // tq
    heads = range(HEADS_PER_STEP)
    hcols = lambda hh: slice(hh * 128, (hh + 1) * 128)
    qrows = lambda qi: pl.ds(pl.multiple_of(qi * tq, tq), tq)

    feat = lax.broadcasted_iota(jnp.int32, (128, tq), 0)
    zero = jnp.zeros((128, tq), F32)
    ones = jnp.ones((BF16_SUBLANES, TK), BF16)
    lv = lamv_ref[...]
    lam = (jnp.exp(jnp.sum(lv[0:1] * lv[1:2], axis=-1, keepdims=True))
           - jnp.exp(jnp.sum(lv[2:3] * lv[3:4], axis=-1, keepdims=True)) + LAMBDA_INIT)
    out_gain = sg_ref[...] * (1.0 - LAMBDA_INIT)

    def stacked_qt(qi, hh):
        qt = q_ref[0, qrows(qi), hcols(hh)].astype(F32).T
        return jnp.concatenate([jnp.where(feat < DA_HEAD_DIM, qt, zero),
                                jnp.where(feat >= DA_HEAD_DIM, qt, zero)], axis=1).astype(BF16)

    cb = ATTN_COL_BLOCK
    subs = range(2 * tq // cb)

    def pieces(masked, sub):
        c0 = sub * cb
        if not masked:
            return [(slice(c0, c0 + cb), TK)]
        return [(slice(c0, c0 + cb // 2), TK // 2), (slice(c0 + cb // 2, c0 + cb), TK)]

    def scores(hh, tile, s_ref, masked, sub):
        start = pl.multiple_of(tile * TK, TK)
        maxes = []
        for cols, nk in pieces(masked, sub):
            s = _mm(k_ref[0, pl.ds(start, nk), hcols(hh)], qqt_ref[hh, :, cols])
            if masked:
                s = s + bias_ref[0:nk, cols.start % tq:(cols.start % tq) + cb // 2]
            s_ref[0:nk, cols] = s
            maxes.append(jnp.max(s, axis=0, keepdims=True))
        return jnp.concatenate(maxes, axis=1)

    def accumulate(hh, tile, s_ref, masked, s_max, m, sub):
        start = pl.multiple_of(tile * TK, TK)
        m_new = jnp.maximum(m, s_max)
        alpha = jnp.exp2(m - m_new)
        for cols, nk in pieces(masked, sub):
            blk = slice(cols.start - sub * cb, cols.stop - sub * cb)
            vt = jnp.concatenate([vt_ref[0, hcols(hh), pl.ds(start, nk)], ones[:, 0:nk]], axis=0)
            p = jnp.exp2(s_ref[0:nk, cols] - m_new[:, blk]).astype(BF16)
            acc_ref[hh, :, cols] = alpha[:, blk] * acc_ref[hh, :, cols] + _mm(vt, p)
        return m_new

    def phase(score_args, acc_args, s_maxes, ms):
        new_maxes, new_ms = [], []
        for sub in subs:
            if score_args is not None:
                new_maxes.append(scores(*score_args, sub))
            new_ms.append(accumulate(*acc_args, s_maxes[sub], ms[sub], sub))
        return tuple(new_maxes), tuple(new_ms)

    def finalize(qi, hh):
        o = acc_ref[hh, 0:DA_V_DIM, :] * (1.0 / acc_ref[hh, DA_V_DIM:DA_V_DIM + 1, :])
        o = o[:, :tq] - lam * o[:, tq:]
        o = o * lax.rsqrt(jnp.mean(o * o, axis=0, keepdims=True) + NORM_EPS) * out_gain
        o_ref[0, qrows(qi), hcols(hh)] = o.T.astype(o_ref.dtype)
        acc_ref[hh] = jnp.zeros(acc_ref.shape[1:], F32)

    m0 = (jnp.full((1, cb), -jnp.inf, F32),) * len(subs)

    def hand_over(qi, sb_max, mb):
        nxt = jnp.minimum(qi + 1, n_qtiles - 1)
        sa_max, _ = phase((0, nxt, sa_ref, True), (1, qi, sb_ref, True), sb_max, mb)
        qqt_ref[1] = stacked_qt(nxt, 1)
        finalize(qi, 1)
        return tuple(reversed(phase((1, 0, sb_ref, False), (0, nxt, sa_ref, True), sa_max, m0)))

    def q_tile(qi, carry):
        ma, sb_max = carry

        def body(j, carry):
            ma, mb, sb_max = carry
            sa_max, mb = phase((0, j, sa_ref, False), (1, j, sb_ref, False), sb_max, mb)
            sb_max, ma = phase((1, j + 1, sb_ref, False), (0, j, sa_ref, False), sa_max, ma)
            return ma, mb, sb_max

        n_loop = qi - 1
        carry = lax.fori_loop(0, n_loop // 2, lambda i, c: body(2 * i + 1, body(2 * i, c)),
                              (ma, m0, sb_max))
        ma, mb, sb_max = lax.cond(n_loop % 2 == 1, lambda c: body(n_loop - 1, c), lambda c: c,
                                  carry)
        sa_max, mb = phase((0, qi - 1, sa_ref, False), (1, qi - 1, sb_ref, False), sb_max, mb)
        sb_max, _ = phase((1, qi, sb_ref, True), (0, qi - 1, sa_ref, False), sa_max, ma)
        qqt_ref[0] = stacked_qt(jnp.minimum(qi + 1, n_qtiles - 1), 0)
        finalize(qi, 0)
        return hand_over(qi, sb_max, mb)

    acc_ref[...] = jnp.zeros_like(acc_ref)
    for hh in heads:
        qqt_ref[hh] = stacked_qt(0, hh)
    sa_max = tuple(scores(0, 0, sa_ref, True, sub) for sub in subs)
    sb_max, _ = phase((1, 0, sb_ref, True), (0, 0, sa_ref, True), sa_max, m0)
    qqt_ref[0] = stacked_qt(1, 0)
    finalize(0, 0)
    lax.fori_loop(1, n_qtiles, q_tile, hand_over(0, sb_max, m0))


def _diff_attn(q, k, vt, bias, lamv, subln_g_col, side_weights):
    b, s, _ = q.shape
    hw = 128 * HEADS_PER_STEP
    n_pairs = DA_HEADS // HEADS_PER_STEP
    side_in, side_out, side_shapes = _side_cast_specs(
        side_weights, b * n_pairs, lambda bi, hi: bi * n_pairs + hi)
    outs = pl.pallas_call(
        _attn_kernel,
        grid=(b, n_pairs),
        in_specs=[
            pl.BlockSpec((1, s, hw), lambda bi, hi: (bi, 0, hi)),
            pl.BlockSpec((1, s, hw), lambda bi, hi: (bi, 0, hi)),
            pl.BlockSpec((1, hw, s), lambda bi, hi: (bi, hi, 0)),
            _const_spec((TK, TQ)),
            _const_spec((4, DA_HEAD_DIM)),
            _const_spec((DA_V_DIM, 1)),
        ] + side_in,
        out_specs=[pl.BlockSpec((1, s, hw), lambda bi, hi: (bi, 0, hi))] + side_out,
        out_shape=[jax.ShapeDtypeStruct((b, s, V_WIDTH), BF16)] + side_shapes,
        scratch_shapes=[pltpu.VMEM((HEADS_PER_STEP, DA_V_DIM + BF16_SUBLANES, 2 * TQ), F32),
                        pltpu.VMEM((TK, 2 * TQ), F32),
                        pltpu.VMEM((TK, 2 * TQ), F32),
                        pltpu.VMEM((HEADS_PER_STEP, 128, 2 * TQ), BF16)],
        name="diff_attn",
        compiler_params=pltpu.CompilerParams(
            dimension_semantics=("arbitrary", "arbitrary"),
            vmem_limit_bytes=V7X_VMEM_LIMIT_BYTES),
    )(q, k, vt, bias, lamv, subln_g_col, *side_weights)
    return outs[0], outs[1:]


def _rope_tables(seq, q_scale):
    half = DA_HEAD_DIM // 2
    inv_freq = 1.0 / (ROPE_THETA ** (np.arange(0, DA_HEAD_DIM, 2, dtype=np.float64) / DA_HEAD_DIM))
    ang = np.arange(seq, dtype=np.float64)[:, None] * inv_freq[None, :]
    cos, sin = np.cos(ang), np.sin(ang)
    cos128 = np.tile(cos, (1, 128 // half))
    sin128 = np.tile(np.concatenate([-sin, sin], axis=1), (1, 128 // DA_HEAD_DIM))
    const = lambda a: jnp.asarray(a.astype(np.float32))
    return const(cos128 * q_scale), const(sin128 * q_scale), const(cos128), const(sin128)


def kernel(x, norm_ffn1, ffn1_gate, ffn1_up, ffn1_down, norm_mix, w_in, b_gate, conv_w,
           w_conv_out, lambda_q1, lambda_k1, lambda_q2, lambda_k2, subln_g, w_attn_out, w_o,
           norm_ffn2, ffn2_gate, ffn2_up, ffn2_down, norm_final):
    bsz, seq, d = x.shape
    assert d == D_MODEL and norm_ffn1.shape[0] == 1
    assert seq % TQ == 0 and seq % TM_PROJ == 0
    assert (bsz * seq) % TM_FFN == 0 and (bsz * seq) % TM_MERGE_FFN == 0
    assert TQ == TK and TQ % CHUNK == 0
    assert HEADS_PER_STEP == 2 and DA_HEADS % HEADS_PER_STEP == 0
    assert ATTN_COL_BLOCK == TQ
    t = bsz * seq
    l = 0
    bf = lambda a: a.astype(BF16)
    row = lambda a: a.reshape(1, -1)

    cosq, sinq, cosk, sink = _rope_tables(seq, math.log2(math.e) / math.sqrt(DA_HEAD_DIM))
    pos_chunk = np.arange(TQ) // CHUNK
    bias = jnp.asarray(np.where(pos_chunk[:, None] <= pos_chunk[None, :], 0.0, -np.inf)
                       .astype(np.float32))
    lamv = jnp.stack([lambda_q1[l], lambda_k1[l], lambda_q2[l], lambda_k2[l]]).astype(F32)

    h, (w_in_bf, w_conv_out_bf) = _ffn(
        x.reshape(t, d), row(norm_ffn1[l]), bf(ffn1_gate[l]), bf(ffn1_up[l]), bf(ffn1_down[l]),
        side_weights=[w_in[l], w_conv_out[l]])
    gy, ga, q, k, vt = _in_proj(
        h.reshape(bsz, seq, d), row(norm_mix[l]), w_in_bf, row(b_gate[l]), conv_w[l],
        w_conv_out_bf, cosq, sinq, cosk, sink)
    o, (w_attn_out_bf, w_o_bf, gate2_bf, up2_bf, down2_bf) = _diff_attn(
        q, k, vt, bias, lamv, subln_g[l].reshape(-1, 1),
        side_weights=[w_attn_out[l], w_o[l], ffn2_gate[l], ffn2_up[l], ffn2_down[l]])
    out = _merge_ffn(h, o.reshape(t, V_WIDTH), gy.reshape(t, d), ga.reshape(t, d),
                     w_attn_out_bf, w_o_bf, row(norm_ffn2[l]), gate2_bf, up2_bf, down2_bf,
                     row(norm_final))
    return out.reshape(bsz, seq, d)
```

```python
import math

import jax
import jax.numpy as jnp
import numpy as np
from jax import lax
from jax.experimental import pallas as pl
from jax.experimental.pallas import tpu as pltpu

D_MODEL = 1024
CHUNK = 64
CONV_WIDTH = 512
CONV_K = 3
DA_HEADS = 4
DA_HEAD_DIM = 64
DA_V_DIM = 128
QK_WIDTH = 512
V_WIDTH = 512
D_FF = 2816
ROPE_THETA = 10000.0
NORM_EPS = 1e-6
LAMBDA_INIT = 0.8 - 0.6 * math.exp(-0.3 * 0)

OFF_XC, OFF_BG, OFF_CG, OFF_Q, OFF_K, OFF_V, OFF_GC, OFF_GA = (
    0, 512, 1024, 1536, 2048, 2560, 3072, 4096)
IN_WIDTH = 5120

V7X_VMEM_LIMIT_BYTES = 56 * 1024 * 1024
BF16_SUBLANES = 16

TM_FFN = 1024
TM_MERGE_FFN = 1024
MERGE_FFN_SLAB = 256
TM_PROJ = 1024
TQ = 512
TK = 512
HEADS_PER_STEP = 2
ATTN_COL_BLOCK = 512
CONV_HALO = 8

BF16 = jnp.bfloat16
F32 = jnp.float32


def _const_spec(shape):
    return pl.BlockSpec(shape, lambda *_: (0,) * len(shape), pipeline_mode=pl.Buffered(1))


def _rms(x, g):
    return x * lax.rsqrt(jnp.mean(x * x, axis=-1, keepdims=True) + NORM_EPS) * g


def _sigmoid(x):
    return 0.5 * jnp.tanh(0.5 * x) + 0.5


def _mm(a, b):
    return jnp.dot(a, b, preferred_element_type=F32)


def _side_cast_specs(weights, n_steps, step_index):
    in_specs, out_specs, out_shapes = [], [], []
    for w in weights:
        rows, cols = w.shape
        assert rows % (n_steps * BF16_SUBLANES) == 0
        spec = pl.BlockSpec((rows // n_steps, cols), lambda *idx: (step_index(*idx), 0))
        in_specs.append(spec)
        out_specs.append(spec)
        out_shapes.append(jax.ShapeDtypeStruct(w.shape, BF16))
    return in_specs, out_specs, out_shapes


def _side_cast(src_refs, dst_refs):
    for src, dst in zip(src_refs, dst_refs):
        dst[...] = src[...].astype(BF16)


def _ffn_kernel(x_ref, g_ref, wg_ref, wu_ref, wd_ref, *rest):
    n_side = len(rest) // 2
    o_ref = rest[n_side]
    _side_cast(rest[:n_side], rest[n_side + 1:])
    x = x_ref[...]
    xn = _rms(x, g_ref[...]).astype(BF16)
    gate = _mm(xn, wg_ref[...])
    up = _mm(xn, wu_ref[...])
    a = (gate * _sigmoid(gate) * up).astype(BF16)
    o_ref[...] = x + 0.5 * _mm(a, wd_ref[...])


def _merge_ffn_kernel(h_ref, o_ref, gy_ref, ga_ref, wao_ref, wo_ref, g_ref, wg_ref, wu_ref,
                      wd_ref, gf_ref, out_ref):
    slab = MERGE_FFN_SLAB
    n_slabs = h_ref.shape[0] // slab

    def merge(r):
        y_attn = _mm(o_ref[r, :], wao_ref[...])
        mix = gy_ref[r, :].astype(F32) + ga_ref[r, :].astype(F32) * y_attn
        return h_ref[r, :] + _mm(mix.astype(BF16), wo_ref[...])

    def gate_up(xn):
        return _mm(xn, wg_ref[...]), _mm(xn, wu_ref[...])

    def down(h2, gate, up):
        a = (gate * _sigmoid(gate) * up).astype(BF16)
        return h2 + 0.5 * _mm(a, wd_ref[...])

    norm = lambda x: _rms(x, g_ref[...]).astype(BF16)
    for pair in range(n_slabs // 2):
        ra, rb = (slice(s * slab, (s + 1) * slab) for s in (2 * pair, 2 * pair + 1))
        h2a = merge(ra)
        h2b = merge(rb)
        gu_a = gate_up(norm(h2a))
        ya = down(h2a, *gu_a)
        gu_b = gate_up(norm(h2b))
        out_ref[ra, :] = _rms(ya, gf_ref[...])
        yb = down(h2b, *gu_b)
        out_ref[rb, :] = _rms(yb, gf_ref[...])


_FFN_WEIGHT_SPECS = lambda: [
    _const_spec((1, D_MODEL)),
    _const_spec((D_MODEL, D_FF)),
    _const_spec((D_MODEL, D_FF)),
    _const_spec((D_FF, D_MODEL)),
]


def _ffn(x2d, g, wg, wu, wd, side_weights):
    t = x2d.shape[0]
    n_steps = t // TM_FFN
    tok = lambda i: (i, 0)
    side_in, side_out, side_shapes = _side_cast_specs(side_weights, n_steps, lambda i: i)
    outs = pl.pallas_call(
        _ffn_kernel,
        grid=(n_steps,),
        in_specs=[pl.BlockSpec((TM_FFN, D_MODEL), tok)] + _FFN_WEIGHT_SPECS() + side_in,
        out_specs=[pl.BlockSpec((TM_FFN, D_MODEL), tok)] + side_out,
        out_shape=[jax.ShapeDtypeStruct((t, D_MODEL), F32)] + side_shapes,
        name="ffn",
        compiler_params=pltpu.CompilerParams(
            dimension_semantics=("arbitrary",), vmem_limit_bytes=V7X_VMEM_LIMIT_BYTES),
    )(x2d, g, wg, wu, wd, *side_weights)
    return outs[0], outs[1:]


def _merge_ffn(h2d, o2d, gy2d, ga2d, wao, wo, g, wg, wu, wd, gf):
    t = h2d.shape[0]
    tm = TM_MERGE_FFN
    tok = lambda i: (i, 0)
    return pl.pallas_call(
        _merge_ffn_kernel,
        grid=(t // tm,),
        in_specs=[
            pl.BlockSpec((tm, D_MODEL), tok),
            pl.BlockSpec((tm, V_WIDTH), tok),
            pl.BlockSpec((tm, D_MODEL), tok),
            pl.BlockSpec((tm, D_MODEL), tok),
            _const_spec((V_WIDTH, D_MODEL)),
            _const_spec((D_MODEL, D_MODEL)),
        ] + _FFN_WEIGHT_SPECS() + [_const_spec((1, D_MODEL))],
        out_specs=pl.BlockSpec((tm, D_MODEL), tok),
        out_shape=jax.ShapeDtypeStruct((t, D_MODEL), F32),
        name="merge_ffn_final",
        compiler_params=pltpu.CompilerParams(
            dimension_semantics=("arbitrary",), vmem_limit_bytes=V7X_VMEM_LIMIT_BYTES),
    )(h2d, o2d, gy2d, ga2d, wao, wo, g, wg, wu, wd, gf)


def _rope(x, cos, sin_signed, lo_half):
    outs = []
    for c in range(x.shape[1] // 128):
        xc = x[:, c * 128:(c + 1) * 128]
        rot = jnp.where(lo_half, pltpu.roll(xc, 96, 1), pltpu.roll(xc, 32, 1))
        outs.append(xc * cos + rot * sin_signed)
    return jnp.concatenate(outs, axis=1)


def _in_proj_kernel(h_ref, g_ref, w_ref, bgate_ref, convw_ref, wco_ref, cosq_ref, sinq_ref,
                    cosk_ref, sink_ref, gy_ref, ga_ref, q_ref, k_ref, vt_ref, zbuf_ref):
    tm = h_ref.shape[1]

    @pl.when(pl.program_id(1) == 0)
    def _():
        zbuf_ref[0:CONV_HALO, :] = jnp.zeros((CONV_HALO, CONV_WIDTH), F32)

    u = _rms(h_ref[0], g_ref[...]).astype(BF16)

    def proj(off, width):
        return _mm(u, w_ref[:, off:off + width])

    vt_ref[0] = proj(OFF_V, V_WIDTH).T.astype(BF16)
    lane = lax.broadcasted_iota(jnp.int32, (tm, 128), 1)
    lo_half = (lane % DA_HEAD_DIM) < (DA_HEAD_DIM // 2)
    q_ref[0] = _rope(proj(OFF_Q, QK_WIDTH), cosq_ref[...], sinq_ref[...], lo_half).astype(BF16)
    k_ref[0] = _rope(proj(OFF_K, QK_WIDTH), cosk_ref[...], sink_ref[...], lo_half).astype(BF16)

    ga_ref[0] = _sigmoid(proj(OFF_GA, D_MODEL) + bgate_ref[:, D_MODEL:]).astype(ga_ref.dtype)
    g_conv = _sigmoid(proj(OFF_GC, D_MODEL) + bgate_ref[:, 0:D_MODEL])

    z = proj(OFF_CG, CONV_WIDTH) * proj(OFF_XC, CONV_WIDTH)
    zbuf_ref[CONV_HALO:CONV_HALO + tm, :] = z
    cw = convw_ref[...]
    zc = (cw[0:1, :] * zbuf_ref[CONV_HALO - 2:CONV_HALO - 2 + tm, :]
          + cw[1:2, :] * zbuf_ref[CONV_HALO - 1:CONV_HALO - 1 + tm, :]
          + cw[2:3, :] * z)
    zbuf_ref[0:CONV_HALO, :] = z[tm - CONV_HALO:, :]
    ya = _mm((proj(OFF_BG, CONV_WIDTH) * zc).astype(BF16), wco_ref[...])
    gy_ref[0] = (g_conv * ya).astype(gy_ref.dtype)


def _in_proj(h3d, g, w_in, b_gate, conv_w, w_conv_out, cosq, sinq, cosk, sink):
    b, s, _ = h3d.shape
    tm = TM_PROJ
    tok = lambda bi, si: (bi, si, 0)
    tab = lambda bi, si: (si, 0)
    bsd = lambda width, dt: jax.ShapeDtypeStruct((b, s, width), dt)
    return pl.pallas_call(
        _in_proj_kernel,
        grid=(b, s // tm),
        in_specs=[
            pl.BlockSpec((1, tm, D_MODEL), tok),
            _const_spec((1, D_MODEL)),
            _const_spec((D_MODEL, IN_WIDTH)),
            _const_spec((1, 2 * D_MODEL)),
            _const_spec((CONV_K, CONV_WIDTH)),
            _const_spec((CONV_WIDTH, D_MODEL)),
            pl.BlockSpec((tm, 128), tab),
            pl.BlockSpec((tm, 128), tab),
            pl.BlockSpec((tm, 128), tab),
            pl.BlockSpec((tm, 128), tab),
        ],
        out_specs=[
            pl.BlockSpec((1, tm, D_MODEL), tok),
            pl.BlockSpec((1, tm, D_MODEL), tok),
            pl.BlockSpec((1, tm, QK_WIDTH), tok),
            pl.BlockSpec((1, tm, QK_WIDTH), tok),
            pl.BlockSpec((1, V_WIDTH, tm), lambda bi, si: (bi, 0, si)),
        ],
        out_shape=[bsd(D_MODEL, BF16), bsd(D_MODEL, BF16), bsd(QK_WIDTH, BF16),
                   bsd(QK_WIDTH, BF16), jax.ShapeDtypeStruct((b, V_WIDTH, s), BF16)],
        scratch_shapes=[pltpu.VMEM((CONV_HALO + tm, CONV_WIDTH), F32)],
        name="in_proj",
        compiler_params=pltpu.CompilerParams(
            dimension_semantics=("arbitrary", "arbitrary"),
            vmem_limit_bytes=V7X_VMEM_LIMIT_BYTES),
    )(h3d, g, w_in, b_gate, conv_w, w_conv_out, cosq, sinq, cosk, sink)


def _attn_kernel(q_ref, k_ref, vt_ref, bias_ref, lamv_ref, sg_ref, *rest):
    n_side = (len(rest) - 5) // 2
    o_ref = rest[n_side]
    acc_ref, sa_ref, sb_ref, qqt_ref = rest[2 * n_side + 1:]
    _side_cast(rest[:n_side], rest[n_side + 1:2 * n_side + 1])
    tq = TQ
    n_qtiles = q_ref.shape[1] // tq
    heads = range(HEADS_PER_STEP)
    hcols = lambda hh: slice(hh * 128, (hh + 1) * 128)
    qrows = lambda qi: pl.ds(pl.multiple_of(qi * tq, tq), tq)

    feat = lax.broadcasted_iota(jnp.int32, (128, tq), 0)
    zero = jnp.zeros((128, tq), F32)
    ones = jnp.ones((BF16_SUBLANES, TK), BF16)
    lv = lamv_ref[...]
    lam = (jnp.exp(jnp.sum(lv[0:1] * lv[1:2], axis=-1, keepdims=True))
           - jnp.exp(jnp.sum(lv[2:3] * lv[3:4], axis=-1, keepdims=True)) + LAMBDA_INIT)
    out_gain = sg_ref[...] * (1.0 - LAMBDA_INIT)

    def stacked_qt(qi, hh):
        qt = q_ref[0, qrows(qi), hcols(hh)].astype(F32).T
        return jnp.concatenate([jnp.where(feat < DA_HEAD_DIM, qt, zero),
                                jnp.where(feat >= DA_HEAD_DIM, qt, zero)], axis=1).astype(BF16)

    cb = ATTN_COL_BLOCK
    subs = range(2 * tq // cb)

    def pieces(masked, sub):
        c0 = sub * cb
        if not masked:
            return [(slice(c0, c0 + cb), TK)]
        return [(slice(c0, c0 + cb // 2), TK // 2), (slice(c0 + cb // 2, c0 + cb), TK)]

    def scores(hh, tile, s_ref, masked, sub):
        start = pl.multiple_of(tile * TK, TK)
        maxes = []
        for cols, nk in pieces(masked, sub):
            s = _mm(k_ref[0, pl.ds(start, nk), hcols(hh)], qqt_ref[hh, :, cols])
            if masked:
                s = s + bias_ref[0:nk, cols.start % tq:(cols.start % tq) + cb // 2]
            s_ref[0:nk, cols] = s
            maxes.append(jnp.max(s, axis=0, keepdims=True))
        return jnp.concatenate(maxes, axis=1)

    def accumulate(hh, tile, s_ref, masked, s_max, m, sub):
        start = pl.multiple_of(tile * TK, TK)
        m_new = jnp.maximum(m, s_max)
        alpha = jnp.exp2(m - m_new)
        for cols, nk in pieces(masked, sub):
            blk = slice(cols.start - sub * cb, cols.stop - sub * cb)
            vt = jnp.concatenate([vt_ref[0, hcols(hh), pl.ds(start, nk)], ones[:, 0:nk]], axis=0)
            p = jnp.exp2(s_ref[0:nk, cols] - m_new[:, blk]).astype(BF16)
            acc_ref[hh, :, cols] = alpha[:, blk] * acc_ref[hh, :, cols] + _mm(vt, p)
        return m_new

    def phase(score_args, acc_args, s_maxes, ms):
        new_maxes, new_ms = [], []
        for sub in subs:
            if score_args is not None:
                new_maxes.append(scores(*score_args, sub))
            new_ms.append(accumulate(*acc_args, s_maxes[sub], ms[sub], sub))
        return tuple(new_maxes), tuple(new_ms)

    def finalize(qi, hh):
        o = acc_ref[hh, 0:DA_V_DIM, :] * (1.0 / acc_ref[hh, DA_V_DIM:DA_V_DIM + 1, :])
        o = o[:, :tq] - lam * o[:, tq:]
        o = o * lax.rsqrt(jnp.mean(o * o, axis=0, keepdims=True) + NORM_EPS)
        o_ref[0, qrows(qi), hcols(hh)] = (o.T * out_gain).astype(o_ref.dtype)
        acc_ref[hh] = jnp.zeros(acc_ref.shape[1:], F32)

    m0 = (jnp.full((1, cb), -jnp.inf, F32),) * len(subs)

    def hand_over(qi, sb_max, mb):
        nxt = jnp.minimum(qi + 1, n_qtiles - 1)
        sa_max, _ = phase((0, nxt, sa_ref, True), (1, qi, sb_ref, True), sb_max, mb)
        qqt_ref[1] = stacked_qt(nxt, 1)
        finalize(qi, 1)
        return tuple(reversed(phase((1, 0, sb_ref, False), (0, nxt, sa_ref, True), sa_max, m0)))

    def q_tile(qi, carry):
        ma, sb_max = carry

        def body(j, carry):
            ma, mb, sb_max = carry
            sa_max, mb = phase((0, j, sa_ref, False), (1, j, sb_ref, False), sb_max, mb)
            sb_max, ma = phase((1, j + 1, sb_ref, False), (0, j, sa_ref, False), sa_max, ma)
            return ma, mb, sb_max

        n_loop = qi - 1
        carry = lax.fori_loop(0, n_loop // 2, lambda i, c: body(2 * i + 1, body(2 * i, c)),
                              (ma, m0, sb_max))
        ma, mb, sb_max = lax.cond(n_loop % 2 == 1, lambda c: body(n_loop - 1, c), lambda c: c,
                                  carry)
        sa_max, mb = phase((0, qi - 1, sa_ref, False), (1, qi - 1, sb_ref, False), sb_max, mb)
        sb_max, _ = phase((1, qi, sb_ref, True), (0, qi - 1, sa_ref, False), sa_max, ma)
        qqt_ref[0] = stacked_qt(jnp.minimum(qi + 1, n_qtiles - 1), 0)
        finalize(qi, 0)
        return hand_over(qi, sb_max, mb)

    acc_ref[...] = jnp.zeros_like(acc_ref)
    for hh in heads:
        qqt_ref[hh] = stacked_qt(0, hh)
    sa_max = tuple(scores(0, 0, sa_ref, True, sub) for sub in subs)
    sb_max, _ = phase((1, 0, sb_ref, True), (0, 0, sa_ref, True), sa_max, m0)
    qqt_ref[0] = stacked_qt(1, 0)
    finalize(0, 0)
    lax.fori_loop(1, n_qtiles, q_tile, hand_over(0, sb_max, m0))


def _diff_attn(q, k, vt, bias, lamv, subln_g_row, side_weights):
    b, s, _ = q.shape
    hw = 128 * HEADS_PER_STEP
    n_pairs = DA_HEADS // HEADS_PER_STEP
    side_in, side_out, side_shapes = _side_cast_specs(
        side_weights, b * n_pairs, lambda bi, hi: bi * n_pairs + hi)
    outs = pl.pallas_call(
        _attn_kernel,
        grid=(b, n_pairs),
        in_specs=[
            pl.BlockSpec((1, s, hw), lambda bi, hi: (bi, 0, hi)),
            pl.BlockSpec((1, s, hw), lambda bi, hi: (bi, 0, hi)),
            pl.BlockSpec((1, hw, s), lambda bi, hi: (bi, hi, 0)),
            _const_spec((TK, TQ)),
            _const_spec((4, DA_HEAD_DIM)),
            _const_spec((1, DA_V_DIM)),
        ] + side_in,
        out_specs=[pl.BlockSpec((1, s, hw), lambda bi, hi: (bi, 0, hi))] + side_out,
        out_shape=[jax.ShapeDtypeStruct((b, s, V_WIDTH), BF16)] + side_shapes,
        scratch_shapes=[pltpu.VMEM((HEADS_PER_STEP, DA_V_DIM + BF16_SUBLANES, 2 * TQ), F32),
                        pltpu.VMEM((TK, 2 * TQ), F32),
                        pltpu.VMEM((TK, 2 * TQ), F32),
                        pltpu.VMEM((HEADS_PER_STEP, 128, 2 * TQ), BF16)],
        name="diff_attn",
        compiler_params=pltpu.CompilerParams(
            dimension_semantics=("arbitrary", "arbitrary"),
            vmem_limit_bytes=V7X_VMEM_LIMIT_BYTES),
    )(q, k, vt, bias, lamv, subln_g_row, *side_weights)
    return outs[0], outs[1:]


def _rope_tables(seq, q_scale):
    half = DA_HEAD_DIM // 2
    inv_freq = 1.0 / (ROPE_THETA ** (np.arange(0, DA_HEAD_DIM, 2, dtype=np.float64) / DA_HEAD_DIM))
    ang = np.arange(seq, dtype=np.float64)[:, None] * inv_freq[None, :]
    cos, sin = np.cos(ang), np.sin(ang)
    cos128 = np.tile(cos, (1, 128 // half))
    sin128 = np.tile(np.concatenate([-sin, sin], axis=1), (1, 128 // DA_HEAD_DIM))
    const = lambda a: jnp.asarray(a.astype(np.float32))
    return const(cos128 * q_scale), const(sin128 * q_scale), const(cos128), const(sin128)


def kernel(x, norm_ffn1, ffn1_gate, ffn1_up, ffn1_down, norm_mix, w_in, b_gate, conv_w,
           w_conv_out, lambda_q1, lambda_k1, lambda_q2, lambda_k2, subln_g, w_attn_out, w_o,
           norm_ffn2, ffn2_gate, ffn2_up, ffn2_down, norm_final):
    bsz, seq, d = x.shape
    assert d == D_MODEL and norm_ffn1.shape[0] == 1
    assert seq % TQ == 0 and seq % TM_PROJ == 0
    assert (bsz * seq) % TM_FFN == 0 and (bsz * seq) % TM_MERGE_FFN == 0
    assert TM_MERGE_FFN % (2 * MERGE_FFN_SLAB) == 0
    assert TQ == TK and TQ % CHUNK == 0
    assert HEADS_PER_STEP == 2 and DA_HEADS % HEADS_PER_STEP == 0
    assert ATTN_COL_BLOCK == TQ
    t = bsz * seq
    l = 0
    bf = lambda a: a.astype(BF16)
    row = lambda a: a.reshape(1, -1)

    cosq, sinq, cosk, sink = _rope_tables(seq, math.log2(math.e) / math.sqrt(DA_HEAD_DIM))
    pos_chunk = np.arange(TQ) // CHUNK
    bias = jnp.asarray(np.where(pos_chunk[:, None] <= pos_chunk[None, :], 0.0, -np.inf)
                       .astype(np.float32))
    lamv = jnp.stack([lambda_q1[l], lambda_k1[l], lambda_q2[l], lambda_k2[l]]).astype(F32)

    h, (w_in_bf, w_conv_out_bf) = _ffn(
        x.reshape(t, d), row(norm_ffn1[l]), bf(ffn1_gate[l]), bf(ffn1_up[l]), bf(ffn1_down[l]),
        side_weights=[w_in[l], w_conv_out[l]])
    gy, ga, q, k, vt = _in_proj(
        h.reshape(bsz, seq, d), row(norm_mix[l]), w_in_bf, row(b_gate[l]), conv_w[l],
        w_conv_out_bf, cosq, sinq, cosk, sink)
    o, (w_attn_out_bf, w_o_bf, gate2_bf, up2_bf, down2_bf) = _diff_attn(
        q, k, vt, bias, lamv, row(subln_g[l]),
        side_weights=[w_attn_out[l], w_o[l], ffn2_gate[l], ffn2_up[l], ffn2_down[l]])
    out = _merge_ffn(h, o.reshape(t, V_WIDTH), gy.reshape(t, d), ga.reshape(t, d),
                     w_attn_out_bf, w_o_bf, row(norm_ffn2[l]), gate2_bf, up2_bf, down2_bf,
                     row(norm_final))
    return out.reshape(bsz, seq, d)
```

```python
import math

import jax
import jax.numpy as jnp
import numpy as np
from jax import lax
from jax.experimental import pallas as pl
from jax.experimental.pallas import tpu as pltpu

D_MODEL = 1024
CHUNK = 64
CONV_WIDTH = 512
CONV_K = 3
DA_HEADS = 4
DA_HEAD_DIM = 64
DA_V_DIM = 128
QK_WIDTH = 512
V_WIDTH = 512
D_FF = 2816
ROPE_THETA = 10000.0
NORM_EPS = 1e-6
LAMBDA_INIT = 0.8 - 0.6 * math.exp(-0.3 * 0)

OFF_XC, OFF_BG, OFF_CG, OFF_Q, OFF_K, OFF_V, OFF_GC, OFF_GA = (
    0, 512, 1024, 1536, 2048, 2560, 3072, 4096)
IN_WIDTH = 5120

V7X_VMEM_LIMIT_BYTES = 56 * 1024 * 1024
BF16_SUBLANES = 16

TM_FFN = 1024
TM_MERGE_FFN = 1024
MERGE_FFN_SLAB = 256
TM_PROJ = 1024
TQ = 512
TK = 512
HEADS_PER_STEP = 2
ATTN_COL_BLOCK = 512
CONV_HALO = 8

BF16 = jnp.bfloat16
F32 = jnp.float32


def _const_spec(shape):
    return pl.BlockSpec(shape, lambda *_: (0,) * len(shape), pipeline_mode=pl.Buffered(1))


def _rms(x, g):
    return x * lax.rsqrt(jnp.mean(x * x, axis=-1, keepdims=True) + NORM_EPS) * g


def _sigmoid(x):
    return 0.5 * jnp.tanh(0.5 * x) + 0.5


def _mm(a, b):
    return jnp.dot(a, b, preferred_element_type=F32)


def _side_cast_specs(weights, n_steps, step_index):
    in_specs, out_specs, out_shapes = [], [], []
    for w in weights:
        rows, cols = w.shape
        assert rows % (n_steps * BF16_SUBLANES) == 0
        spec = pl.BlockSpec((rows // n_steps, cols), lambda *idx: (step_index(*idx), 0))
        in_specs.append(spec)
        out_specs.append(spec)
        out_shapes.append(jax.ShapeDtypeStruct(w.shape, BF16))
    return in_specs, out_specs, out_shapes


def _side_cast(src_refs, dst_refs):
    for src, dst in zip(src_refs, dst_refs):
        dst[...] = src[...].astype(BF16)


def _ffn_kernel(x_ref, g_ref, wg_ref, wu_ref, wd_ref, *rest):
    n_side = len(rest) // 2
    o_ref = rest[n_side]
    _side_cast(rest[:n_side], rest[n_side + 1:])
    slab = MERGE_FFN_SLAB
    norm = lambda r: _rms(x_ref[r, :], g_ref[...]).astype(BF16)

    def gate_up(xn):
        return _mm(xn, wg_ref[...]), _mm(xn, wu_ref[...])

    def finish(r, gate, up):
        a = (gate * _sigmoid(gate) * up).astype(BF16)
        o_ref[r, :] = x_ref[r, :] + 0.5 * _mm(a, wd_ref[...])

    for pair in range(x_ref.shape[0] // (2 * slab)):
        ra, rb = (slice(s * slab, (s + 1) * slab) for s in (2 * pair, 2 * pair + 1))
        gu_a = gate_up(norm(ra))
        xnb = norm(rb)
        finish(ra, *gu_a)
        finish(rb, *gate_up(xnb))


def _merge_ffn_kernel(h_ref, o_ref, gy_ref, ga_ref, wao_ref, wo_ref, g_ref, wg_ref, wu_ref,
                      wd_ref, gf_ref, out_ref):
    slab = MERGE_FFN_SLAB
    n_slabs = h_ref.shape[0] // slab

    def merge(r):
        y_attn = _mm(o_ref[r, :], wao_ref[...])
        mix = gy_ref[r, :].astype(F32) + ga_ref[r, :].astype(F32) * y_attn
        return h_ref[r, :] + _mm(mix.astype(BF16), wo_ref[...])

    def gate_up(xn):
        return _mm(xn, wg_ref[...]), _mm(xn, wu_ref[...])

    def down(h2, gate, up):
        a = (gate * _sigmoid(gate) * up).astype(BF16)
        return h2 + 0.5 * _mm(a, wd_ref[...])

    norm = lambda x: _rms(x, g_ref[...]).astype(BF16)
    for pair in range(n_slabs // 2):
        ra, rb = (slice(s * slab, (s + 1) * slab) for s in (2 * pair, 2 * pair + 1))
        h2a = merge(ra)
        h2b = merge(rb)
        gu_a = gate_up(norm(h2a))
        ya = down(h2a, *gu_a)
        gu_b = gate_up(norm(h2b))
        out_ref[ra, :] = _rms(ya, gf_ref[...])
        yb = down(h2b, *gu_b)
        out_ref[rb, :] = _rms(yb, gf_ref[...])


_FFN_WEIGHT_SPECS = lambda: [
    _const_spec((1, D_MODEL)),
    _const_spec((D_MODEL, D_FF)),
    _const_spec((D_MODEL, D_FF)),
    _const_spec((D_FF, D_MODEL)),
]


def _ffn(x2d, g, wg, wu, wd, side_weights):
    t = x2d.shape[0]
    n_steps = t // TM_FFN
    tok = lambda i: (i, 0)
    side_in, side_out, side_shapes = _side_cast_specs(side_weights, n_steps, lambda i: i)
    outs = pl.pallas_call(
        _ffn_kernel,
        grid=(n_steps,),
        in_specs=[pl.BlockSpec((TM_FFN, D_MODEL), tok)] + _FFN_WEIGHT_SPECS() + side_in,
        out_specs=[pl.BlockSpec((TM_FFN, D_MODEL), tok)] + side_out,
        out_shape=[jax.ShapeDtypeStruct((t, D_MODEL), F32)] + side_shapes,
        name="ffn",
        compiler_params=pltpu.CompilerParams(
            dimension_semantics=("arbitrary",), vmem_limit_bytes=V7X_VMEM_LIMIT_BYTES),
    )(x2d, g, wg, wu, wd, *side_weights)
    return outs[0], outs[1:]


def _merge_ffn(h2d, o2d, gy2d, ga2d, wao, wo, g, wg, wu, wd, gf):
    t = h2d.shape[0]
    tm = TM_MERGE_FFN
    tok = lambda i: (i, 0)
    return pl.pallas_call(
        _merge_ffn_kernel,
        grid=(t // tm,),
        in_specs=[
            pl.BlockSpec((tm, D_MODEL), tok),
            pl.BlockSpec((tm, V_WIDTH), tok),
            pl.BlockSpec((tm, D_MODEL), tok),
            pl.BlockSpec((tm, D_MODEL), tok),
            _const_spec((V_WIDTH, D_MODEL)),
            _const_spec((D_MODEL, D_MODEL)),
        ] + _FFN_WEIGHT_SPECS() + [_const_spec((1, D_MODEL))],
        out_specs=pl.BlockSpec((tm, D_MODEL), tok),
        out_shape=jax.ShapeDtypeStruct((t, D_MODEL), F32),
        name="merge_ffn_final",
        compiler_params=pltpu.CompilerParams(
            dimension_semantics=("arbitrary",), vmem_limit_bytes=V7X_VMEM_LIMIT_BYTES),
    )(h2d, o2d, gy2d, ga2d, wao, wo, g, wg, wu, wd, gf)


def _rope(x, cos, sin_signed, lo_half):
    outs = []
    for c in range(x.shape[1] // 128):
        xc = x[:, c * 128:(c + 1) * 128]
        rot = jnp.where(lo_half, pltpu.roll(xc, 96, 1), pltpu.roll(xc, 32, 1))
        outs.append(xc * cos + rot * sin_signed)
    return jnp.concatenate(outs, axis=1)


def _in_proj_kernel(h_ref, g_ref, w_ref, bgate_ref, convw_ref, wco_ref, cosq_ref, sinq_ref,
                    cosk_ref, sink_ref, gy_ref, ga_ref, q_ref, k_ref, vt_ref, zbuf_ref):
    tm = h_ref.shape[1]

    @pl.when(pl.program_id(1) == 0)
    def _():
        zbuf_ref[0:CONV_HALO, :] = jnp.zeros((CONV_HALO, CONV_WIDTH), F32)

    u = _rms(h_ref[0], g_ref[...]).astype(BF16)

    def proj(off, width):
        return _mm(u, w_ref[:, off:off + width])

    vt_ref[0] = proj(OFF_V, V_WIDTH).T.astype(BF16)
    lane = lax.broadcasted_iota(jnp.int32, (tm, 128), 1)
    lo_half = (lane % DA_HEAD_DIM) < (DA_HEAD_DIM // 2)
    q_ref[0] = _rope(proj(OFF_Q, QK_WIDTH), cosq_ref[...], sinq_ref[...], lo_half).astype(BF16)
    k_ref[0] = _rope(proj(OFF_K, QK_WIDTH), cosk_ref[...], sink_ref[...], lo_half).astype(BF16)

    ga_ref[0] = _sigmoid(proj(OFF_GA, D_MODEL) + bgate_ref[:, D_MODEL:]).astype(ga_ref.dtype)
    g_conv = _sigmoid(proj(OFF_GC, D_MODEL) + bgate_ref[:, 0:D_MODEL])

    z = proj(OFF_CG, CONV_WIDTH) * proj(OFF_XC, CONV_WIDTH)
    zbuf_ref[CONV_HALO:CONV_HALO + tm, :] = z
    cw = convw_ref[...]
    zc = (cw[0:1, :] * zbuf_ref[CONV_HALO - 2:CONV_HALO - 2 + tm, :]
          + cw[1:2, :] * zbuf_ref[CONV_HALO - 1:CONV_HALO - 1 + tm, :]
          + cw[2:3, :] * z)
    zbuf_ref[0:CONV_HALO, :] = z[tm - CONV_HALO:, :]
    ya = _mm((proj(OFF_BG, CONV_WIDTH) * zc).astype(BF16), wco_ref[...])
    gy_ref[0] = (g_conv * ya).astype(gy_ref.dtype)


def _in_proj(h3d, g, w_in, b_gate, conv_w, w_conv_out, cosq, sinq, cosk, sink):
    b, s, _ = h3d.shape
    tm = TM_PROJ
    tok = lambda bi, si: (bi, si, 0)
    tab = lambda bi, si: (si, 0)
    bsd = lambda width, dt: jax.ShapeDtypeStruct((b, s, width), dt)
    return pl.pallas_call(
        _in_proj_kernel,
        grid=(b, s // tm),
        in_specs=[
            pl.BlockSpec((1, tm, D_MODEL), tok),
            _const_spec((1, D_MODEL)),
            _const_spec((D_MODEL, IN_WIDTH)),
            _const_spec((1, 2 * D_MODEL)),
            _const_spec((CONV_K, CONV_WIDTH)),
            _const_spec((CONV_WIDTH, D_MODEL)),
            pl.BlockSpec((tm, 128), tab),
            pl.BlockSpec((tm, 128), tab),
            pl.BlockSpec((tm, 128), tab),
            pl.BlockSpec((tm, 128), tab),
        ],
        out_specs=[
            pl.BlockSpec((1, tm, D_MODEL), tok),
            pl.BlockSpec((1, tm, D_MODEL), tok),
            pl.BlockSpec((1, tm, QK_WIDTH), tok),
            pl.BlockSpec((1, tm, QK_WIDTH), tok),
            pl.BlockSpec((1, V_WIDTH, tm), lambda bi, si: (bi, 0, si)),
        ],
        out_shape=[bsd(D_MODEL, BF16), bsd(D_MODEL, BF16), bsd(QK_WIDTH, BF16),
                   bsd(QK_WIDTH, BF16), jax.ShapeDtypeStruct((b, V_WIDTH, s), BF16)],
        scratch_shapes=[pltpu.VMEM((CONV_HALO + tm, CONV_WIDTH), F32)],
        name="in_proj",
        compiler_params=pltpu.CompilerParams(
            dimension_semantics=("arbitrary", "arbitrary"),
            vmem_limit_bytes=V7X_VMEM_LIMIT_BYTES),
    )(h3d, g, w_in, b_gate, conv_w, w_conv_out, cosq, sinq, cosk, sink)


def _attn_kernel(q_ref, k_ref, vt_ref, bias_ref, lamv_ref, sg_ref, *rest):
    n_side = (len(rest) - 5) // 2
    o_ref = rest[n_side]
    acc_ref, sa_ref, sb_ref, qqt_ref = rest[2 * n_side + 1:]
    _side_cast(rest[:n_side], rest[n_side + 1:2 * n_side + 1])
    tq = TQ
    n_qtiles = q_ref.shape[1] // tq
    heads = range(HEADS_PER_STEP)
    hcols = lambda hh: slice(hh * 128, (hh + 1) * 128)
    qrows = lambda qi: pl.ds(pl.multiple_of(qi * tq, tq), tq)

    feat = lax.broadcasted_iota(jnp.int32, (128, tq), 0)
    zero = jnp.zeros((128, tq), F32)
    ones = jnp.ones((BF16_SUBLANES, TK), BF16)
    lv = lamv_ref[...]
    lam = (jnp.exp(jnp.sum(lv[0:1] * lv[1:2], axis=-1, keepdims=True))
           - jnp.exp(jnp.sum(lv[2:3] * lv[3:4], axis=-1, keepdims=True)) + LAMBDA_INIT)
    out_gain = sg_ref[...] * (1.0 - LAMBDA_INIT)

    def stacked_qt(qi, hh):
        qt = q_ref[0, qrows(qi), hcols(hh)].astype(F32).T
        return jnp.concatenate([jnp.where(feat < DA_HEAD_DIM, qt, zero),
                                jnp.where(feat >= DA_HEAD_DIM, qt, zero)], axis=1).astype(BF16)

    cb = ATTN_COL_BLOCK
    subs = range(2 * tq // cb)

    def pieces(masked, sub):
        c0 = sub * cb
        if not masked:
            return [(slice(c0, c0 + cb), TK)]
        return [(slice(c0, c0 + cb // 2), TK // 2), (slice(c0 + cb // 2, c0 + cb), TK)]

    def scores(hh, tile, s_ref, masked, sub):
        start = pl.multiple_of(tile * TK, TK)
        maxes = []
        for cols, nk in pieces(masked, sub):
            s = _mm(k_ref[0, pl.ds(start, nk), hcols(hh)], qqt_ref[hh, :, cols])
            if masked:
                s = s + bias_ref[0:nk, cols.start % tq:(cols.start % tq) + cb // 2]
            s_ref[0:nk, cols] = s
            maxes.append(jnp.max(s, axis=0, keepdims=True))
        return jnp.concatenate(maxes, axis=1)

    def accumulate(hh, tile, s_ref, masked, s_max, m, sub):
        start = pl.multiple_of(tile * TK, TK)
        m_new = jnp.maximum(m, s_max)
        alpha = jnp.exp2(m - m_new)
        for cols, nk in pieces(masked, sub):
            blk = slice(cols.start - sub * cb, cols.stop - sub * cb)
            vt = jnp.concatenate([vt_ref[0, hcols(hh), pl.ds(start, nk)], ones[:, 0:nk]], axis=0)
            p = jnp.exp2(s_ref[0:nk, cols] - m_new[:, blk]).astype(BF16)
            acc_ref[hh, :, cols] = alpha[:, blk] * acc_ref[hh, :, cols] + _mm(vt, p)
        return m_new

    def phase(score_args, acc_args, s_maxes, ms):
        new_maxes, new_ms = [], []
        for sub in subs:
            if score_args is not None:
                new_maxes.append(scores(*score_args, sub))
            new_ms.append(accumulate(*acc_args, s_maxes[sub], ms[sub], sub))
        return tuple(new_maxes), tuple(new_ms)

    def finalize(qi, hh):
        o = acc_ref[hh, 0:DA_V_DIM, :] * (1.0 / acc_ref[hh, DA_V_DIM:DA_V_DIM + 1, :])
        o = o[:, :tq] - lam * o[:, tq:]
        o = o * lax.rsqrt(jnp.mean(o * o, axis=0, keepdims=True) + NORM_EPS)
        o_ref[0, qrows(qi), hcols(hh)] = (o.T * out_gain).astype(o_ref.dtype)
        acc_ref[hh] = jnp.zeros(acc_ref.shape[1:], F32)

    m0 = (jnp.full((1, cb), -jnp.inf, F32),) * len(subs)

    def hand_over(qi, sb_max, mb):
        nxt = jnp.minimum(qi + 1, n_qtiles - 1)
        sa_max, _ = phase((0, nxt, sa_ref, True), (1, qi, sb_ref, True), sb_max, mb)
        qqt_ref[1] = stacked_qt(nxt, 1)
        finalize(qi, 1)
        return tuple(reversed(phase((1, 0, sb_ref, False), (0, nxt, sa_ref, True), sa_max, m0)))

    def q_tile(qi, carry):
        ma, sb_max = carry

        def body(j, carry):
            ma, mb, sb_max = carry
            sa_max, mb = phase((0, j, sa_ref, False), (1, j, sb_ref, False), sb_max, mb)
            sb_max, ma = phase((1, j + 1, sb_ref, False), (0, j, sa_ref, False), sa_max, ma)
            return ma, mb, sb_max

        n_loop = qi - 1
        carry = lax.fori_loop(0, n_loop // 2, lambda i, c: body(2 * i + 1, body(2 * i, c)),
                              (ma, m0, sb_max))
        ma, mb, sb_max = lax.cond(n_loop % 2 == 1, lambda c: body(n_loop - 1, c), lambda c: c,
                                  carry)
        sa_max, mb = phase((0, qi - 1, sa_ref, False), (1, qi - 1, sb_ref, False), sb_max, mb)
        sb_max, _ = phase((1, qi, sb_ref, True), (0, qi - 1, sa_ref, False), sa_max, ma)
        qqt_ref[0] = stacked_qt(jnp.minimum(qi + 1, n_qtiles - 1), 0)
        finalize(qi, 0)
        return hand_over(qi, sb_max, mb)

    acc_ref[...] = jnp.zeros_like(acc_ref)
    for hh in heads:
        qqt_ref[hh] = stacked_qt(0, hh)
    sa_max = tuple(scores(0, 0, sa_ref, True, sub) for sub in subs)
    sb_max, _ = phase((1, 0, sb_ref, True), (0, 0, sa_ref, True), sa_max, m0)
    qqt_ref[0] = stacked_qt(1, 0)
    finalize(0, 0)
    lax.fori_loop(1, n_qtiles, q_tile, hand_over(0, sb_max, m0))


def _diff_attn(q, k, vt, bias, lamv, subln_g_row, side_weights):
    b, s, _ = q.shape
    hw = 128 * HEADS_PER_STEP
    n_pairs = DA_HEADS // HEADS_PER_STEP
    side_in, side_out, side_shapes = _side_cast_specs(
        side_weights, b * n_pairs, lambda bi, hi: bi * n_pairs + hi)
    outs = pl.pallas_call(
        _attn_kernel,
        grid=(b, n_pairs),
        in_specs=[
            pl.BlockSpec((1, s, hw), lambda bi, hi: (bi, 0, hi)),
            pl.BlockSpec((1, s, hw), lambda bi, hi: (bi, 0, hi)),
            pl.BlockSpec((1, hw, s), lambda bi, hi: (bi, hi, 0)),
            _const_spec((TK, TQ)),
            _const_spec((4, DA_HEAD_DIM)),
            _const_spec((1, DA_V_DIM)),
        ] + side_in,
        out_specs=[pl.BlockSpec((1, s, hw), lambda bi, hi: (bi, 0, hi))] + side_out,
        out_shape=[jax.ShapeDtypeStruct((b, s, V_WIDTH), BF16)] + side_shapes,
        scratch_shapes=[pltpu.VMEM((HEADS_PER_STEP, DA_V_DIM + BF16_SUBLANES, 2 * TQ), F32),
                        pltpu.VMEM((TK, 2 * TQ), F32),
                        pltpu.VMEM((TK, 2 * TQ), F32),
                        pltpu.VMEM((HEADS_PER_STEP, 128, 2 * TQ), BF16)],
        name="diff_attn",
        compiler_params=pltpu.CompilerParams(
            dimension_semantics=("arbitrary", "arbitrary"),
            vmem_limit_bytes=V7X_VMEM_LIMIT_BYTES),
    )(q, k, vt, bias, lamv, subln_g_row, *side_weights)
    return outs[0], outs[1:]


def _rope_tables(seq, q_scale):
    half = DA_HEAD_DIM // 2
    inv_freq = 1.0 / (ROPE_THETA ** (np.arange(0, DA_HEAD_DIM, 2, dtype=np.float64) / DA_HEAD_DIM))
    ang = np.arange(seq, dtype=np.float64)[:, None] * inv_freq[None, :]
    cos, sin = np.cos(ang), np.sin(ang)
    cos128 = np.tile(cos, (1, 128 // half))
    sin128 = np.tile(np.concatenate([-sin, sin], axis=1), (1, 128 // DA_HEAD_DIM))
    const = lambda a: jnp.asarray(a.astype(np.float32))
    return const(cos128 * q_scale), const(sin128 * q_scale), const(cos128), const(sin128)


def kernel(x, norm_ffn1, ffn1_gate, ffn1_up, ffn1_down, norm_mix, w_in, b_gate, conv_w,
           w_conv_out, lambda_q1, lambda_k1, lambda_q2, lambda_k2, subln_g, w_attn_out, w_o,
           norm_ffn2, ffn2_gate, ffn2_up, ffn2_down, norm_final):
    bsz, seq, d = x.shape
    assert d == D_MODEL and norm_ffn1.shape[0] == 1
    assert seq % TQ == 0 and seq % TM_PROJ == 0
    assert (bsz * seq) % TM_FFN == 0 and (bsz * seq) % TM_MERGE_FFN == 0
    assert TM_MERGE_FFN % (2 * MERGE_FFN_SLAB) == 0
    assert TQ == TK and TQ % CHUNK == 0
    assert HEADS_PER_STEP == 2 and DA_HEADS % HEADS_PER_STEP == 0
    assert ATTN_COL_BLOCK == TQ
    t = bsz * seq
    l = 0
    bf = lambda a: a.astype(BF16)
    row = lambda a: a.reshape(1, -1)

    cosq, sinq, cosk, sink = _rope_tables(seq, math.log2(math.e) / math.sqrt(DA_HEAD_DIM))
    pos_chunk = np.arange(TQ) // CHUNK
    bias = jnp.asarray(np.where(pos_chunk[:, None] <= pos_chunk[None, :], 0.0, -np.inf)
                       .astype(np.float32))
    lamv = jnp.stack([lambda_q1[l], lambda_k1[l], lambda_q2[l], lambda_k2[l]]).astype(F32)

    h, (w_in_bf, w_conv_out_bf) = _ffn(
        x.reshape(t, d), row(norm_ffn1[l]), bf(ffn1_gate[l]), bf(ffn1_up[l]), bf(ffn1_down[l]),
        side_weights=[w_in[l], w_conv_out[l]])
    gy, ga, q, k, vt = _in_proj(
        h.reshape(bsz, seq, d), row(norm_mix[l]), w_in_bf, row(b_gate[l]), conv_w[l],
        w_conv_out_bf, cosq, sinq, cosk, sink)
    o, (w_attn_out_bf, w_o_bf, gate2_bf, up2_bf, down2_bf) = _diff_attn(
        q, k, vt, bias, lamv, row(subln_g[l]),
        side_weights=[w_attn_out[l], w_o[l], ffn2_gate[l], ffn2_up[l], ffn2_down[l]])
    out = _merge_ffn(h, o.reshape(t, V_WIDTH), gy.reshape(t, d), ga.reshape(t, d),
                     w_attn_out_bf, w_o_bf, row(norm_ffn2[l]), gate2_bf, up2_bf, down2_bf,
                     row(norm_final))
    return out.reshape(bsz, seq, d)
```

```python
import math

import jax
import jax.numpy as jnp
import numpy as np
from jax import lax
from jax.experimental import pallas as pl
from jax.experimental.pallas import tpu as pltpu

D_MODEL = 1024
CHUNK = 64
CONV_WIDTH = 512
CONV_K = 3
DA_HEADS = 4
DA_HEAD_DIM = 64
DA_V_DIM = 128
QK_WIDTH = 512
V_WIDTH = 512
D_FF = 2816
ROPE_THETA = 10000.0
NORM_EPS = 1e-6
LAMBDA_INIT = 0.8 - 0.6 * math.exp(-0.3 * 0)

OFF_XC, OFF_BG, OFF_CG, OFF_Q, OFF_K, OFF_V, OFF_GC, OFF_GA = (
    0, 512, 1024, 1536, 2048, 2560, 3072, 4096)
IN_WIDTH = 5120

V7X_VMEM_LIMIT_BYTES = 56 * 1024 * 1024
BF16_SUBLANES = 16

TM_FFN = 1024
TM_MERGE_FFN = 1024
MERGE_FFN_SLAB = 256
FF_CHUNKS = ((0, 1536), (1536, 2816))
TM_PROJ = 1024
TQ = 512
TK = 512
HEADS_PER_STEP = 2
ATTN_COL_BLOCK = 512
CONV_HALO = 8

BF16 = jnp.bfloat16
F32 = jnp.float32


def _const_spec(shape):
    return pl.BlockSpec(shape, lambda *_: (0,) * len(shape), pipeline_mode=pl.Buffered(1))


def _rms(x, g):
    return x * lax.rsqrt(jnp.mean(x * x, axis=-1, keepdims=True) + NORM_EPS) * g


def _sigmoid(x):
    return 0.5 * jnp.tanh(0.5 * x) + 0.5


def _mm(a, b):
    return jnp.dot(a, b, preferred_element_type=F32)


def _side_cast_specs(weights, n_steps, step_index):
    in_specs, out_specs, out_shapes = [], [], []
    for w in weights:
        rows, cols = w.shape
        assert rows % (n_steps * BF16_SUBLANES) == 0
        spec = pl.BlockSpec((rows // n_steps, cols), lambda *idx: (step_index(*idx), 0))
        in_specs.append(spec)
        out_specs.append(spec)
        out_shapes.append(jax.ShapeDtypeStruct(w.shape, BF16))
    return in_specs, out_specs, out_shapes


def _side_cast(src_refs, dst_refs):
    for src, dst in zip(src_refs, dst_refs):
        dst[...] = src[...].astype(BF16)


def _ffn_kernel(x_ref, g_ref, wg_ref, wu_ref, wd_ref, *rest):
    n_side = len(rest) // 2
    o_ref = rest[n_side]
    _side_cast(rest[:n_side], rest[n_side + 1:])
    slab = MERGE_FFN_SLAB
    norm = lambda r: _rms(x_ref[r, :], g_ref[...]).astype(BF16)

    def swiglu(xn):
        total = None
        for c0, c1 in FF_CHUNKS:
            gate = _mm(xn, wg_ref[:, c0:c1])
            up = _mm(xn, wu_ref[:, c0:c1])
            a = (gate * _sigmoid(gate) * up).astype(BF16)
            part = _mm(a, wd_ref[c0:c1, :])
            total = part if total is None else total + part
        return total

    for pair in range(x_ref.shape[0] // (2 * slab)):
        ra, rb = (slice(s * slab, (s + 1) * slab) for s in (2 * pair, 2 * pair + 1))
        xna = norm(ra)
        xnb = norm(rb)
        o_ref[ra, :] = x_ref[ra, :] + 0.5 * swiglu(xna)
        o_ref[rb, :] = x_ref[rb, :] + 0.5 * swiglu(xnb)


def _merge_ffn_kernel(h_ref, o_ref, gy_ref, ga_ref, wao_ref, wo_ref, g_ref, wg_ref, wu_ref,
                      wd_ref, gf_ref, out_ref):
    slab = MERGE_FFN_SLAB
    n_slabs = h_ref.shape[0] // slab

    def merge(r):
        y_attn = _mm(o_ref[r, :], wao_ref[...])
        mix = gy_ref[r, :].astype(F32) + ga_ref[r, :].astype(F32) * y_attn
        return h_ref[r, :] + _mm(mix.astype(BF16), wo_ref[...])

    def gate_up(xn):
        return _mm(xn, wg_ref[...]), _mm(xn, wu_ref[...])

    def down(h2, gate, up):
        a = (gate * _sigmoid(gate) * up).astype(BF16)
        return h2 + 0.5 * _mm(a, wd_ref[...])

    norm = lambda x: _rms(x, g_ref[...]).astype(BF16)
    for pair in range(n_slabs // 2):
        ra, rb = (slice(s * slab, (s + 1) * slab) for s in (2 * pair, 2 * pair + 1))
        h2a = merge(ra)
        h2b = merge(rb)
        gu_a = gate_up(norm(h2a))
        ya = down(h2a, *gu_a)
        gu_b = gate_up(norm(h2b))
        out_ref[ra, :] = _rms(ya, gf_ref[...])
        yb = down(h2b, *gu_b)
        out_ref[rb, :] = _rms(yb, gf_ref[...])


_FFN_WEIGHT_SPECS = lambda: [
    _const_spec((1, D_MODEL)),
    _const_spec((D_MODEL, D_FF)),
    _const_spec((D_MODEL, D_FF)),
    _const_spec((D_FF, D_MODEL)),
]


def _ffn(x2d, g, wg, wu, wd, side_weights):
    t = x2d.shape[0]
    n_steps = t // TM_FFN
    tok = lambda i: (i, 0)
    side_in, side_out, side_shapes = _side_cast_specs(side_weights, n_steps, lambda i: i)
    outs = pl.pallas_call(
        _ffn_kernel,
        grid=(n_steps,),
        in_specs=[pl.BlockSpec((TM_FFN, D_MODEL), tok)] + _FFN_WEIGHT_SPECS() + side_in,
        out_specs=[pl.BlockSpec((TM_FFN, D_MODEL), tok)] + side_out,
        out_shape=[jax.ShapeDtypeStruct((t, D_MODEL), F32)] + side_shapes,
        name="ffn",
        compiler_params=pltpu.CompilerParams(
            dimension_semantics=("arbitrary",), vmem_limit_bytes=V7X_VMEM_LIMIT_BYTES),
    )(x2d, g, wg, wu, wd, *side_weights)
    return outs[0], outs[1:]


def _merge_ffn(h2d, o2d, gy2d, ga2d, wao, wo, g, wg, wu, wd, gf):
    t = h2d.shape[0]
    tm = TM_MERGE_FFN
    tok = lambda i: (i, 0)
    return pl.pallas_call(
        _merge_ffn_kernel,
        grid=(t // tm,),
        in_specs=[
            pl.BlockSpec((tm, D_MODEL), tok),
            pl.BlockSpec((tm, V_WIDTH), tok),
            pl.BlockSpec((tm, D_MODEL), tok),
            pl.BlockSpec((tm, D_MODEL), tok),
            _const_spec((V_WIDTH, D_MODEL)),
            _const_spec((D_MODEL, D_MODEL)),
        ] + _FFN_WEIGHT_SPECS() + [_const_spec((1, D_MODEL))],
        out_specs=pl.BlockSpec((tm, D_MODEL), tok),
        out_shape=jax.ShapeDtypeStruct((t, D_MODEL), F32),
        name="merge_ffn_final",
        compiler_params=pltpu.CompilerParams(
            dimension_semantics=("arbitrary",), vmem_limit_bytes=V7X_VMEM_LIMIT_BYTES),
    )(h2d, o2d, gy2d, ga2d, wao, wo, g, wg, wu, wd, gf)


def _rope(x, cos, sin_signed, lo_half):
    outs = []
    for c in range(x.shape[1] // 128):
        xc = x[:, c * 128:(c + 1) * 128]
        rot = jnp.where(lo_half, pltpu.roll(xc, 96, 1), pltpu.roll(xc, 32, 1))
        outs.append(xc * cos + rot * sin_signed)
    return jnp.concatenate(outs, axis=1)


def _in_proj_kernel(h_ref, g_ref, w_ref, bgate_ref, convw_ref, wco_ref, cosq_ref, sinq_ref,
                    cosk_ref, sink_ref, gy_ref, ga_ref, q_ref, k_ref, vt_ref, zbuf_ref):
    tm = h_ref.shape[1]

    @pl.when(pl.program_id(1) == 0)
    def _():
        zbuf_ref[0:CONV_HALO, :] = jnp.zeros((CONV_HALO, CONV_WIDTH), F32)

    u = _rms(h_ref[0], g_ref[...]).astype(BF16)

    def proj(off, width):
        return _mm(u, w_ref[:, off:off + width])

    vt_ref[0] = proj(OFF_V, V_WIDTH).T.astype(BF16)
    lane = lax.broadcasted_iota(jnp.int32, (tm, 128), 1)
    lo_half = (lane % DA_HEAD_DIM) < (DA_HEAD_DIM // 2)
    q_ref[0] = _rope(proj(OFF_Q, QK_WIDTH), cosq_ref[...], sinq_ref[...], lo_half).astype(BF16)
    k_ref[0] = _rope(proj(OFF_K, QK_WIDTH), cosk_ref[...], sink_ref[...], lo_half).astype(BF16)

    ga_ref[0] = _sigmoid(proj(OFF_GA, D_MODEL) + bgate_ref[:, D_MODEL:]).astype(ga_ref.dtype)
    g_conv = _sigmoid(proj(OFF_GC, D_MODEL) + bgate_ref[:, 0:D_MODEL])

    z = proj(OFF_CG, CONV_WIDTH) * proj(OFF_XC, CONV_WIDTH)
    zbuf_ref[CONV_HALO:CONV_HALO + tm, :] = z
    cw = convw_ref[...]
    zc = (cw[0:1, :] * zbuf_ref[CONV_HALO - 2:CONV_HALO - 2 + tm, :]
          + cw[1:2, :] * zbuf_ref[CONV_HALO - 1:CONV_HALO - 1 + tm, :]
          + cw[2:3, :] * z)
    zbuf_ref[0:CONV_HALO, :] = z[tm - CONV_HALO:, :]
    ya = _mm((proj(OFF_BG, CONV_WIDTH) * zc).astype(BF16), wco_ref[...])
    gy_ref[0] = (g_conv * ya).astype(gy_ref.dtype)


def _in_proj(h3d, g, w_in, b_gate, conv_w, w_conv_out, cosq, sinq, cosk, sink):
    b, s, _ = h3d.shape
    tm = TM_PROJ
    tok = lambda bi, si: (bi, si, 0)
    tab = lambda bi, si: (si, 0)
    bsd = lambda width, dt: jax.ShapeDtypeStruct((b, s, width), dt)
    return pl.pallas_call(
        _in_proj_kernel,
        grid=(b, s // tm),
        in_specs=[
            pl.BlockSpec((1, tm, D_MODEL), tok),
            _const_spec((1, D_MODEL)),
            _const_spec((D_MODEL, IN_WIDTH)),
            _const_spec((1, 2 * D_MODEL)),
            _const_spec((CONV_K, CONV_WIDTH)),
            _const_spec((CONV_WIDTH, D_MODEL)),
            pl.BlockSpec((tm, 128), tab),
            pl.BlockSpec((tm, 128), tab),
            pl.BlockSpec((tm, 128), tab),
            pl.BlockSpec((tm, 128), tab),
        ],
        out_specs=[
            pl.BlockSpec((1, tm, D_MODEL), tok),
            pl.BlockSpec((1, tm, D_MODEL), tok),
            pl.BlockSpec((1, tm, QK_WIDTH), tok),
            pl.BlockSpec((1, tm, QK_WIDTH), tok),
            pl.BlockSpec((1, V_WIDTH, tm), lambda bi, si: (bi, 0, si)),
        ],
        out_shape=[bsd(D_MODEL, BF16), bsd(D_MODEL, BF16), bsd(QK_WIDTH, BF16),
                   bsd(QK_WIDTH, BF16), jax.ShapeDtypeStruct((b, V_WIDTH, s), BF16)],
        scratch_shapes=[pltpu.VMEM((CONV_HALO + tm, CONV_WIDTH), F32)],
        name="in_proj",
        compiler_params=pltpu.CompilerParams(
            dimension_semantics=("arbitrary", "arbitrary"),
            vmem_limit_bytes=V7X_VMEM_LIMIT_BYTES),
    )(h3d, g, w_in, b_gate, conv_w, w_conv_out, cosq, sinq, cosk, sink)


def _attn_kernel(q_ref, k_ref, vt_ref, bias_ref, lamv_ref, sg_ref, *rest):
    n_side = (len(rest) - 5) // 2
    o_ref = rest[n_side]
    acc_ref, sa_ref, sb_ref, qqt_ref = rest[2 * n_side + 1:]
    _side_cast(rest[:n_side], rest[n_side + 1:2 * n_side + 1])
    tq = TQ
    n_qtiles = q_ref.shape[1] // tq
    heads = range(HEADS_PER_STEP)
    hcols = lambda hh: slice(hh * 128, (hh + 1) * 128)
    qrows = lambda qi: pl.ds(pl.multiple_of(qi * tq, tq), tq)

    feat = lax.broadcasted_iota(jnp.int32, (128, tq), 0)
    zero = jnp.zeros((128, tq), F32)
    ones = jnp.ones((BF16_SUBLANES, TK), BF16)
    lv = lamv_ref[...]
    lam = (jnp.exp(jnp.sum(lv[0:1] * lv[1:2], axis=-1, keepdims=True))
           - jnp.exp(jnp.sum(lv[2:3] * lv[3:4], axis=-1, keepdims=True)) + LAMBDA_INIT)
    out_gain = sg_ref[...] * (1.0 - LAMBDA_INIT)

    def stacked_qt(qi, hh):
        qt = q_ref[0, qrows(qi), hcols(hh)].astype(F32).T
        return jnp.concatenate([jnp.where(feat < DA_HEAD_DIM, qt, zero),
                                jnp.where(feat >= DA_HEAD_DIM, qt, zero)], axis=1).astype(BF16)

    cb = ATTN_COL_BLOCK
    subs = range(2 * tq // cb)

    def pieces(masked, sub):
        c0 = sub * cb
        if not masked:
            return [(slice(c0, c0 + cb), TK)]
        return [(slice(c0, c0 + cb // 2), TK // 2), (slice(c0 + cb // 2, c0 + cb), TK)]

    def scores(hh, tile, s_ref, masked, sub):
        start = pl.multiple_of(tile * TK, TK)
        maxes = []
        for cols, nk in pieces(masked, sub):
            s = _mm(k_ref[0, pl.ds(start, nk), hcols(hh)], qqt_ref[hh, :, cols])
            if masked:
                s = s + bias_ref[0:nk, cols.start % tq:(cols.start % tq) + cb // 2]
            s_ref[0:nk, cols] = s
            maxes.append(jnp.max(s, axis=0, keepdims=True))
        return jnp.concatenate(maxes, axis=1)

    def accumulate(hh, tile, s_ref, masked, s_max, m, sub):
        start = pl.multiple_of(tile * TK, TK)
        m_new = jnp.maximum(m, s_max)
        alpha = jnp.exp2(m - m_new)
        for cols, nk in pieces(masked, sub):
            blk = slice(cols.start - sub * cb, cols.stop - sub * cb)
            vt = jnp.concatenate([vt_ref[0, hcols(hh), pl.ds(start, nk)], ones[:, 0:nk]], axis=0)
            p = jnp.exp2(s_ref[0:nk, cols] - m_new[:, blk]).astype(BF16)
            acc_ref[hh, :, cols] = alpha[:, blk] * acc_ref[hh, :, cols] + _mm(vt, p)
        return m_new

    def phase(score_args, acc_args, s_maxes, ms):
        new_maxes, new_ms = [], []
        for sub in subs:
            if score_args is not None:
                new_maxes.append(scores(*score_args, sub))
            new_ms.append(accumulate(*acc_args, s_maxes[sub], ms[sub], sub))
        return tuple(new_maxes), tuple(new_ms)

    def finalize(qi, hh):
        o = acc_ref[hh, 0:DA_V_DIM, :] * (1.0 / acc_ref[hh, DA_V_DIM:DA_V_DIM + 1, :])
        o = o[:, :tq] - lam * o[:, tq:]
        o = o * lax.rsqrt(jnp.mean(o * o, axis=0, keepdims=True) + NORM_EPS)
        o_ref[0, qrows(qi), hcols(hh)] = (o.T * out_gain).astype(o_ref.dtype)
        acc_ref[hh] = jnp.zeros(acc_ref.shape[1:], F32)

    m0 = (jnp.full((1, cb), -jnp.inf, F32),) * len(subs)

    def hand_over(qi, sb_max, mb):
        nxt = jnp.minimum(qi + 1, n_qtiles - 1)
        sa_max, _ = phase((0, nxt, sa_ref, True), (1, qi, sb_ref, True), sb_max, mb)
        qqt_ref[1] = stacked_qt(nxt, 1)
        finalize(qi, 1)
        return tuple(reversed(phase((1, 0, sb_ref, False), (0, nxt, sa_ref, True), sa_max, m0)))

    def q_tile(qi, carry):
        ma, sb_max = carry

        def body(j, carry):
            ma, mb, sb_max = carry
            sa_max, mb = phase((0, j, sa_ref, False), (1, j, sb_ref, False), sb_max, mb)
            sb_max, ma = phase((1, j + 1, sb_ref, False), (0, j, sa_ref, False), sa_max, ma)
            return ma, mb, sb_max

        n_loop = qi - 1
        carry = lax.fori_loop(0, n_loop // 2, lambda i, c: body(2 * i + 1, body(2 * i, c)),
                              (ma, m0, sb_max))
        ma, mb, sb_max = lax.cond(n_loop % 2 == 1, lambda c: body(n_loop - 1, c), lambda c: c,
                                  carry)
        sa_max, mb = phase((0, qi - 1, sa_ref, False), (1, qi - 1, sb_ref, False), sb_max, mb)
        sb_max, _ = phase((1, qi, sb_ref, True), (0, qi - 1, sa_ref, False), sa_max, ma)
        qqt_ref[0] = stacked_qt(jnp.minimum(qi + 1, n_qtiles - 1), 0)
        finalize(qi, 0)
        return hand_over(qi, sb_max, mb)

    acc_ref[...] = jnp.zeros_like(acc_ref)
    for hh in heads:
        qqt_ref[hh] = stacked_qt(0, hh)
    sa_max = tuple(scores(0, 0, sa_ref, True, sub) for sub in subs)
    sb_max, _ = phase((1, 0, sb_ref, True), (0, 0, sa_ref, True), sa_max, m0)
    qqt_ref[0] = stacked_qt(1, 0)
    finalize(0, 0)
    lax.fori_loop(1, n_qtiles, q_tile, hand_over(0, sb_max, m0))


def _diff_attn(q, k, vt, bias, lamv, subln_g_row, side_weights):
    b, s, _ = q.shape
    hw = 128 * HEADS_PER_STEP
    n_pairs = DA_HEADS // HEADS_PER_STEP
    side_in, side_out, side_shapes = _side_cast_specs(
        side_weights, b * n_pairs, lambda bi, hi: bi * n_pairs + hi)
    outs = pl.pallas_call(
        _attn_kernel,
        grid=(b, n_pairs),
        in_specs=[
            pl.BlockSpec((1, s, hw), lambda bi, hi: (bi, 0, hi)),
            pl.BlockSpec((1, s, hw), lambda bi, hi: (bi, 0, hi)),
            pl.BlockSpec((1, hw, s), lambda bi, hi: (bi, hi, 0)),
            _const_spec((TK, TQ)),
            _const_spec((4, DA_HEAD_DIM)),
            _const_spec((1, DA_V_DIM)),
        ] + side_in,
        out_specs=[pl.BlockSpec((1, s, hw), lambda bi, hi: (bi, 0, hi))] + side_out,
        out_shape=[jax.ShapeDtypeStruct((b, s, V_WIDTH), BF16)] + side_shapes,
        scratch_shapes=[pltpu.VMEM((HEADS_PER_STEP, DA_V_DIM + BF16_SUBLANES, 2 * TQ), F32),
                        pltpu.VMEM((TK, 2 * TQ), F32),
                        pltpu.VMEM((TK, 2 * TQ), F32),
                        pltpu.VMEM((HEADS_PER_STEP, 128, 2 * TQ), BF16)],
        name="diff_attn",
        compiler_params=pltpu.CompilerParams(
            dimension_semantics=("arbitrary", "arbitrary"),
            vmem_limit_bytes=V7X_VMEM_LIMIT_BYTES),
    )(q, k, vt, bias, lamv, subln_g_row, *side_weights)
    return outs[0], outs[1:]


def _rope_tables(seq, q_scale):
    half = DA_HEAD_DIM // 2
    inv_freq = 1.0 / (ROPE_THETA ** (np.arange(0, DA_HEAD_DIM, 2, dtype=np.float64) / DA_HEAD_DIM))
    ang = np.arange(seq, dtype=np.float64)[:, None] * inv_freq[None, :]
    cos, sin = np.cos(ang), np.sin(ang)
    cos128 = np.tile(cos, (1, 128 // half))
    sin128 = np.tile(np.concatenate([-sin, sin], axis=1), (1, 128 // DA_HEAD_DIM))
    const = lambda a: jnp.asarray(a.astype(np.float32))
    return const(cos128 * q_scale), const(sin128 * q_scale), const(cos128), const(sin128)


def kernel(x, norm_ffn1, ffn1_gate, ffn1_up, ffn1_down, norm_mix, w_in, b_gate, conv_w,
           w_conv_out, lambda_q1, lambda_k1, lambda_q2, lambda_k2, subln_g, w_attn_out, w_o,
           norm_ffn2, ffn2_gate, ffn2_up, ffn2_down, norm_final):
    bsz, seq, d = x.shape
    assert d == D_MODEL and norm_ffn1.shape[0] == 1
    assert seq % TQ == 0 and seq % TM_PROJ == 0
    assert (bsz * seq) % TM_FFN == 0 and (bsz * seq) % TM_MERGE_FFN == 0
    assert TM_MERGE_FFN % (2 * MERGE_FFN_SLAB) == 0
    assert TQ == TK and TQ % CHUNK == 0
    assert HEADS_PER_STEP == 2 and DA_HEADS % HEADS_PER_STEP == 0
    assert ATTN_COL_BLOCK == TQ
    t = bsz * seq
    l = 0
    bf = lambda a: a.astype(BF16)
    row = lambda a: a.reshape(1, -1)

    cosq, sinq, cosk, sink = _rope_tables(seq, math.log2(math.e) / math.sqrt(DA_HEAD_DIM))
    pos_chunk = np.arange(TQ) // CHUNK
    bias = jnp.asarray(np.where(pos_chunk[:, None] <= pos_chunk[None, :], 0.0, -np.inf)
                       .astype(np.float32))
    lamv = jnp.stack([lambda_q1[l], lambda_k1[l], lambda_q2[l], lambda_k2[l]]).astype(F32)

    h, (w_in_bf, w_conv_out_bf) = _ffn(
        x.reshape(t, d), row(norm_ffn1[l]), bf(ffn1_gate[l]), bf(ffn1_up[l]), bf(ffn1_down[l]),
        side_weights=[w_in[l], w_conv_out[l]])
    gy, ga, q, k, vt = _in_proj(
        h.reshape(bsz, seq, d), row(norm_mix[l]), w_in_bf, row(b_gate[l]), conv_w[l],
        w_conv_out_bf, cosq, sinq, cosk, sink)
    o, (w_attn_out_bf, w_o_bf, gate2_bf, up2_bf, down2_bf) = _diff_attn(
        q, k, vt, bias, lamv, row(subln_g[l]),
        side_weights=[w_attn_out[l], w_o[l], ffn2_gate[l], ffn2_up[l], ffn2_down[l]])
    out = _merge_ffn(h, o.reshape(t, V_WIDTH), gy.reshape(t, d), ga.reshape(t, d),
                     w_attn_out_bf, w_o_bf, row(norm_ffn2[l]), gate2_bf, up2_bf, down2_bf,
                     row(norm_final))
    return out.reshape(bsz, seq, d)
```
